```python
import math, functools
import jax, jax.numpy as jnp
from jax import lax
import numpy as np

D_MODEL = 1024
BATCH = 8
SEQ = 8192
DEPTH = 1
DEC_BATCH = 128
DEC_SEQ = 1
PAST_LEN = 8192
PAGE_SIZE = 128

SSM_WIDTH = D_MODEL // 2
SSM_GROUP = 16
SSM_GROUPS = SSM_WIDTH // SSM_GROUP
SSM_STATE = 64
SSM_CHUNK = 128
N_HEADS = 8
HEAD_DIM = 64
ATT_WIDTH = N_HEADS * HEAD_DIM
IDX_HEADS = 8
IDX_DIM = 32
TOPK_MAX = 256
Q_BLOCK = 128
REL_BUCKETS = 32
REL_MAX_DIST = 128
D_FF = -(-8 * D_MODEL // (3 * 256)) * 256
LN_EPS = 1e-5
DEEPNORM_ALPHA = (2 * DEPTH) ** 0.25
DEEPNORM_BETA = (8 * DEPTH) ** -0.25
IN_SPLITS = (SSM_WIDTH, ATT_WIDTH, ATT_WIDTH, ATT_WIDTH, IDX_HEADS * IDX_DIM, IDX_DIM, IDX_HEADS, D_MODEL, D_MODEL)
IN_COLS = sum(IN_SPLITS)

kernel_name = 'hybrid_s5_dsa_gated_decoder_step'


def layer_norm(x, g, b):
    xf = x.astype(jnp.float32)
    mu = jnp.mean(xf, axis=-1, keepdims=True)
    var = jnp.mean(jnp.square(xf - mu), axis=-1, keepdims=True)
    return ((xf - mu) * lax.rsqrt(var + LN_EPS) * g.astype(jnp.float32) + b.astype(jnp.float32)).astype(x.dtype)


def split_proj(p):
    offs = [int(o) for o in np.cumsum(IN_SPLITS)[:-1]]
    return jnp.split(p, offs, axis=-1)


def _scan_op(e1, e2):
    a1, b1 = e1
    a2, b2 = e2
    return a1 * a2, a2 * b1 + b2


def ssm_chunk(h0, u, a_bar, b_bar, c, d):
    bu = jnp.einsum('btgi,gpi->btgp', u.astype(jnp.complex64), b_bar)
    a_cum, b_cum = lax.associative_scan(_scan_op, (jnp.broadcast_to(a_bar, bu.shape), bu), axis=1)
    h = a_cum * h0[:, None] + b_cum
    y = jnp.einsum('btgp,gip->btgi', h, c).real + d * u
    return h[:, -1], y


def ssm_branch(u, h0, a_re, a_im, log_dt, b_re, b_im, c_re, c_im, d, w_glu):
    bsz, t_len, _ = u.shape
    f32 = jnp.float32
    a = lax.complex(a_re.astype(f32), a_im.astype(f32))
    dt = jnp.exp(log_dt.astype(f32))[:, None]
    a_bar = jnp.exp(dt * a)
    b_bar = ((a_bar - 1.0) / a)[:, :, None] * lax.complex(b_re.astype(f32), b_im.astype(f32))
    c = lax.complex(c_re.astype(f32), c_im.astype(f32))
    df = d.astype(f32)
    ch = SSM_CHUNK if t_len % SSM_CHUNK == 0 else t_len
    nc = t_len // ch
    u_chunks = u.astype(f32).reshape(bsz, nc, ch, SSM_GROUPS, SSM_GROUP).swapaxes(0, 1)
    step = functools.partial(ssm_chunk, a_bar=a_bar, b_bar=b_bar, c=c, d=df)
    h_last, y = lax.scan(step, h0, u_chunks)
    y = y.swapaxes(0, 1).reshape(bsz, t_len, SSM_WIDTH).astype(u.dtype)
    y = jax.nn.gelu(y)
    y = y * jax.nn.sigmoid(y @ w_glu)
    return y, h_last


def rel_bucket(rel):
    n = jnp.maximum(rel, 0)
    max_exact = REL_BUCKETS // 2
    nf = jnp.maximum(n, max_exact).astype(jnp.float32)
    large = max_exact + (jnp.log(nf / max_exact) / math.log(REL_MAX_DIST / max_exact)
                         * (REL_BUCKETS - max_exact)).astype(jnp.int32)
    large = jnp.minimum(large, REL_BUCKETS - 1)
    return jnp.where(n < max_exact, n, large)


def indexer_scores(qi, wi, ki):
    s = jax.nn.relu(jnp.einsum('bthd,bld->bthl', qi, ki) * IDX_DIM ** -0.5)
    return jnp.einsum('bthl,bth->btl', s, wi * IDX_HEADS ** -0.5).astype(jnp.float32)


def attend_selected(q, k_sel, v_sel, q_pos, sel_pos, rel_bias):
    logits = jnp.einsum('bthd,btkhd->bthk', q, k_sel).astype(jnp.float32) * HEAD_DIM ** -0.5
    rel = q_pos[None, :, None] - sel_pos
    bias = rel_bias[rel_bucket(rel)]
    logits = logits + jnp.moveaxis(bias, -1, -2).astype(jnp.float32)
    valid = (rel >= 0)[:, :, None, :]
    logits = jnp.where(valid, logits, -jnp.inf)
    p = jax.nn.softmax(logits, axis=-1).astype(v_sel.dtype)
    return jnp.einsum('bthk,btkhd->bthd', p, v_sel)


def _gather_rows(arr, idx):
    return jax.vmap(lambda a, i: a[i])(arr, idx)


def prompt_attention(q, k, v, qi, ki, wi, rel_bias):
    bsz, s_len = q.shape[:2]
    k_sel_n = min(TOPK_MAX, s_len // 4)
    nb = s_len // Q_BLOCK
    key_pos = jnp.arange(s_len, dtype=jnp.int32)

    def block(args):
        qb, qib, wib, qpos = args
        sc = indexer_scores(qib, wib, ki)
        sc = jnp.where(key_pos[None, None, :] <= qpos[None, :, None], sc, -jnp.inf)
        _, idx = lax.top_k(sc, k_sel_n)
        return attend_selected(qb, _gather_rows(k, idx), _gather_rows(v, idx), qpos, idx, rel_bias)

    def to_blocks(t):
        return t.reshape(bsz, nb, Q_BLOCK, *t.shape[2:]).swapaxes(0, 1)

    out = lax.map(block, (to_blocks(q), to_blocks(qi), to_blocks(wi), key_pos.reshape(nb, Q_BLOCK)))
    return out.swapaxes(0, 1).reshape(bsz, s_len, ATT_WIDTH)


def sample_attention(q, k, v, qi, ki, wi, cache_k, cache_v, cache_idx_k, page_table, rel_bias):
    bsz, t_len = q.shape[:2]
    n_pages = page_table.shape[1]
    past = n_pages * PAGE_SIZE
    l_len = past + t_len
    k_sel_n = min(TOPK_MAX, l_len // 4)
    ki_past = cache_idx_k[page_table].reshape(bsz, past, IDX_DIM)
    ki_all = jnp.concatenate([ki_past, ki], axis=1)
    q_pos = past + jnp.arange(t_len, dtype=jnp.int32)
    key_pos = jnp.arange(l_len, dtype=jnp.int32)
    sc = indexer_scores(qi, wi, ki_all)
    sc = jnp.where(key_pos[None, None, :] <= q_pos[None, :, None], sc, -jnp.inf)
    _, idx = lax.top_k(sc, k_sel_n)
    past_idx = jnp.minimum(idx, past - 1)
    phys = jnp.take_along_axis(page_table, (past_idx // PAGE_SIZE).reshape(bsz, -1), axis=1).reshape(idx.shape)
    row = phys * PAGE_SIZE + past_idx % PAGE_SIZE
    new_idx = jnp.clip(idx - past, 0, t_len - 1)
    from_past = (idx < past)[..., None, None]
    k_flat = cache_k.reshape(-1, N_HEADS, HEAD_DIM)
    v_flat = cache_v.reshape(-1, N_HEADS, HEAD_DIM)
    k_sel = jnp.where(from_past, k_flat[row], _gather_rows(k, new_idx))
    v_sel = jnp.where(from_past, v_flat[row], _gather_rows(v, new_idx))
    return attend_selected(q, k_sel, v_sel, q_pos, idx, rel_bias).reshape(bsz, t_len, ATT_WIDTH)


def decoder_layer(x, h0, attn_fn, w_in, a_re, a_im, log_dt, b_re, b_im, c_re, c_im, d, w_glu,
                  w_branch_a, w_branch_b, w_out, ln1_g, ln1_b, w_ffn_in, w_ffn_out, ln2_g, ln2_b):
    bsz, t_len, _ = x.shape
    u, q, k, v, qi, ki, wi, ga, gb = split_proj(x @ w_in)
    q = q.reshape(bsz, t_len, N_HEADS, HEAD_DIM)
    k = k.reshape(bsz, t_len, N_HEADS, HEAD_DIM)
    v = v.reshape(bsz, t_len, N_HEADS, HEAD_DIM)
    qi = qi.reshape(bsz, t_len, IDX_HEADS, IDX_DIM)
    ya, h_last = ssm_branch(u, h0, a_re, a_im, log_dt, b_re, b_im, c_re, c_im, d, w_glu)
    yb = attn_fn(q, k, v, qi, ki, wi)
    merged = jax.nn.sigmoid(ga) * (ya @ w_branch_a) + jax.nn.sigmoid(gb) * (yb @ w_branch_b)
    x1 = layer_norm(DEEPNORM_ALPHA * x + merged @ w_out, ln1_g, ln1_b)
    gate, up = jnp.split(x1 @ w_ffn_in, 2, axis=-1)
    x2 = layer_norm(DEEPNORM_ALPHA * x1 + (jax.nn.silu(gate) * up) @ w_ffn_out, ln2_g, ln2_b)
    return x2, h_last, k, v, ki


def setup_inputs(seed: int = 0) -> dict:
    key = jax.random.key(seed)
    ks = jax.random.split(key, 32)
    n_pages = PAST_LEN // PAGE_SIZE
    n_pool = (DEC_BATCH * n_pages * 5) // 4
    f32 = jnp.float32

    def nrm(k, shape, scale):
        return jax.random.normal(k, shape, f32) * scale

    col_scale = jnp.concatenate([jnp.full((n,), DEEPNORM_BETA if i == 3 else 1.0, f32) for i, n in enumerate(IN_SPLITS)])
    page_table = jax.random.permutation(ks[7], n_pool)[:DEC_BATCH * n_pages].reshape(DEC_BATCH, n_pages).astype(jnp.int32)
    return {
        'x_prompt': nrm(ks[0], (BATCH, SEQ, D_MODEL), 1.0),
        'x_sample': nrm(ks[1], (DEC_BATCH, DEC_SEQ, D_MODEL), 1.0),
        'cache_k': nrm(ks[2], (DEPTH, n_pool, PAGE_SIZE, N_HEADS, HEAD_DIM), 1.0),
        'cache_v': nrm(ks[3], (DEPTH, n_pool, PAGE_SIZE, N_HEADS, HEAD_DIM), 1.0),
        'cache_idx_k': nrm(ks[4], (DEPTH, n_pool, PAGE_SIZE, IDX_DIM), 1.0),
        'state_ssm_re': nrm(ks[5], (DEPTH, DEC_BATCH, SSM_GROUPS, SSM_STATE), 0.5),
        'state_ssm_im': nrm(ks[6], (DEPTH, DEC_BATCH, SSM_GROUPS, SSM_STATE), 0.5),
        'page_table': page_table,
        'w_in': nrm(ks[8], (DEPTH, D_MODEL, IN_COLS), D_MODEL ** -0.5) * col_scale,
        'ssm_a_re': -0.5 + nrm(ks[9], (DEPTH, SSM_GROUPS, SSM_STATE), 0.01),
        'ssm_a_im': math.pi * jnp.arange(SSM_STATE, dtype=f32) + nrm(ks[10], (DEPTH, SSM_GROUPS, SSM_STATE), 0.01),
        'ssm_log_dt': jax.random.uniform(ks[11], (DEPTH, SSM_GROUPS), f32, math.log(0.001), math.log(0.1)),
        'ssm_b_re': nrm(ks[12], (DEPTH, SSM_GROUPS, SSM_STATE, SSM_GROUP), (2 * SSM_GROUP) ** -0.5),
        'ssm_b_im': nrm(ks[13], (DEPTH, SSM_GROUPS, SSM_STATE, SSM_GROUP), (2 * SSM_GROUP) ** -0.5),
        'ssm_c_re': nrm(ks[14], (DEPTH, SSM_GROUPS, SSM_GROUP, SSM_STATE), (2 * SSM_STATE) ** -0.5),
        'ssm_c_im': nrm(ks[15], (DEPTH, SSM_GROUPS, SSM_GROUP, SSM_STATE), (2 * SSM_STATE) ** -0.5),
        'ssm_d': nrm(ks[16], (DEPTH, SSM_GROUPS, SSM_GROUP), 1.0),
        'w_glu': nrm(ks[17], (DEPTH, SSM_WIDTH, SSM_WIDTH), SSM_WIDTH ** -0.5),
        'w_branch_a': nrm(ks[18], (DEPTH, SSM_WIDTH, D_MODEL), SSM_WIDTH ** -0.5),
        'w_branch_b': nrm(ks[19], (DEPTH, ATT_WIDTH, D_MODEL), ATT_WIDTH ** -0.5),
        'rel_bias': nrm(ks[20], (REL_BUCKETS, N_HEADS), 0.5),
        'w_out': nrm(ks[21], (DEPTH, D_MODEL, D_MODEL), D_MODEL ** -0.5) * DEEPNORM_BETA,
        'ln1_g': 1.0 + nrm(ks[22], (DEPTH, D_MODEL), 0.01),
        'ln1_b': nrm(ks[23], (DEPTH, D_MODEL), 0.01),
        'w_ffn_in': nrm(ks[24], (DEPTH, D_MODEL, 2 * D_FF), D_MODEL ** -0.5),
        'w_ffn_out': nrm(ks[25], (DEPTH, D_FF, D_MODEL), D_FF ** -0.5) * DEEPNORM_BETA,
        'ln2_g': 1.0 + nrm(ks[26], (DEPTH, D_MODEL), 0.01),
        'ln2_b': nrm(ks[27], (DEPTH, D_MODEL), 0.01),
    }


def reference(x_prompt, x_sample, cache_k, cache_v, cache_idx_k, state_ssm_re, state_ssm_im, page_table,
              w_in, ssm_a_re, ssm_a_im, ssm_log_dt, ssm_b_re, ssm_b_im, ssm_c_re, ssm_c_im, ssm_d, w_glu,
              w_branch_a, w_branch_b, rel_bias, w_out, ln1_g, ln1_b, w_ffn_in, w_ffn_out, ln2_g, ln2_b):
    x_p, x_s = x_prompt, x_sample
    kp_l, vp_l, kip_l, hpr_l, hpi_l = [], [], [], [], []
    ks_l, vs_l, kis_l, hsr_l, hsi_l = [], [], [], [], []
    for l in range(DEPTH):
        params = (w_in[l], ssm_a_re[l], ssm_a_im[l], ssm_log_dt[l], ssm_b_re[l], ssm_b_im[l],
                  ssm_c_re[l], ssm_c_im[l], ssm_d[l], w_glu[l], w_branch_a[l], w_branch_b[l], w_out[l],
                  ln1_g[l], ln1_b[l], w_ffn_in[l], w_ffn_out[l], ln2_g[l], ln2_b[l])
        h0_p = jnp.zeros((x_p.shape[0], SSM_GROUPS, SSM_STATE), jnp.complex64)
        x_p, hp, kp, vp, kip = decoder_layer(
            x_p, h0_p, functools.partial(prompt_attention, rel_bias=rel_bias), *params)
        h0_s = lax.complex(state_ssm_re[l].astype(jnp.float32), state_ssm_im[l].astype(jnp.float32))
        attn_s = functools.partial(sample_attention, cache_k=cache_k[l], cache_v=cache_v[l],
                                   cache_idx_k=cache_idx_k[l], page_table=page_table, rel_bias=rel_bias)
        x_s, hs, ks_, vs_, kis = decoder_layer(x_s, h0_s, attn_s, *params)
        kp_l.append(kp); vp_l.append(vp); kip_l.append(kip); hpr_l.append(hp.real); hpi_l.append(hp.imag)
        ks_l.append(ks_); vs_l.append(vs_); kis_l.append(kis); hsr_l.append(hs.real); hsi_l.append(hs.imag)
    return (x_p, x_s,
            jnp.stack(kp_l), jnp.stack(vp_l), jnp.stack(kip_l), jnp.stack(hpr_l), jnp.stack(hpi_l),
            jnp.stack(ks_l), jnp.stack(vs_l), jnp.stack(kis_l), jnp.stack(hsr_l), jnp.stack(hsi_l))
```

```python
import functools
import math

import jax
import jax.numpy as jnp
from jax import lax
from jax.experimental import pallas as pl
from jax.experimental.pallas import tpu as pltpu

F32 = jnp.float32
BF16 = jnp.bfloat16
I32 = jnp.int32

D_MODEL = 1024
SSM_WIDTH = 512
SSM_GROUP = 16
SSM_GROUPS = 32
SSM_STATE = 64
SSM_COLS = SSM_GROUPS * SSM_STATE
N_HEADS = 8
HEAD_DIM = 64
ATT_WIDTH = 512
IDX_HEADS = 8
IDX_DIM = 32
TOPK_MAX = 256
PAGE_SIZE = 128
REL_BUCKETS = 32
REL_MAX_DIST = 128
D_FF = 2816
LN_EPS = 1e-5
IN_SPLITS = (SSM_WIDTH, ATT_WIDTH, ATT_WIDTH, ATT_WIDTH, IDX_HEADS * IDX_DIM, IDX_DIM, IDX_HEADS, D_MODEL, D_MODEL)

VMEM_LIMIT_BYTES = 56 * 1024 * 1024
NEG_BIG = -1e30
KEY_NEG_INF = -2139095041
KEY_MAX = 2147483647

TOKEN_TILE = 512
Q_TILE = 256
KV_TILE = 256
SSM_T_TILE = 64
FF_CHUNK = 256
SAMPLE_IDX_PAGES = 16
SAMPLE_KV_PAGES = 8


def _params(sem):
    return pltpu.CompilerParams(dimension_semantics=sem, vmem_limit_bytes=VMEM_LIMIT_BYTES)


def _const_spec(shape):
    n = len(shape)
    return pl.BlockSpec(shape, lambda *_: (0,) * n)


def _resident_spec(block, index_map):
    return pl.BlockSpec(block, index_map, pipeline_mode=pl.Buffered(1))


def _dot(a, b):
    return jnp.dot(a, b, preferred_element_type=F32)


def _dot_nt(a, b):
    return lax.dot_general(a, b, (((1,), (1,)), ((), ())), preferred_element_type=F32)


def _layer_norm(z, g, b):
    mu = jnp.mean(z, axis=-1, keepdims=True)
    zc = z - mu
    var = jnp.mean(zc * zc, axis=-1, keepdims=True)
    return zc * lax.rsqrt(var + LN_EPS) * g + b


def _ordered_key(x):
    b = pltpu.bitcast(x, I32)
    return b ^ ((b >> 31) & KEY_MAX)


def _in_proj_prompt_kernel(x_ref, wu, wk, wv, wga, wgb, wki, wtq, wtv, wtqi, wtwi,
                           u_o, k_o, v_o, ki_o, kbf_o, kibf_o, sga_o, sgb_o, qt_o, vt_o, qit_o, wit_o):
    xb = x_ref[0].astype(BF16)
    u_o[0] = _dot(xb, wu[...])
    k = _dot(xb, wk[...])
    k_o[0] = k
    kbf_o[0] = k.astype(BF16)
    v_o[0] = _dot(xb, wv[...])
    ki = _dot(xb, wki[...])
    ki_o[0] = ki
    kibf_o[0] = ki.astype(BF16)
    sga_o[0] = jax.nn.sigmoid(_dot(xb, wga[...])).astype(BF16)
    sgb_o[0] = jax.nn.sigmoid(_dot(xb, wgb[...])).astype(BF16)
    qt_o[0] = _dot_nt(wtq[...], xb).astype(BF16)
    vt_o[0] = _dot_nt(wtv[...], xb).astype(BF16)
    qit_o[0] = _dot_nt(wtqi[...], xb).astype(BF16)
    wit_o[0] = _dot_nt(wtwi[...], xb)[:IDX_HEADS]


def _in_proj_prompt(x, w):
    bsz, s_len, _ = x.shape
    tm = min(TOKEN_TILE, s_len)
    nat = lambda c: pl.BlockSpec((1, tm, c), lambda b, i: (b, i, 0))
    tr = lambda c: pl.BlockSpec((1, c, tm), lambda b, i: (b, 0, i))
    sds = jax.ShapeDtypeStruct
    weights = [w['u'], w['k'], w['v'], w['ga'], w['gb'], w['ki'], w['tq'], w['tv'], w['tqi'], w['twi']]
    return pl.pallas_call(
        _in_proj_prompt_kernel,
        grid=(bsz, s_len // tm),
        in_specs=[nat(D_MODEL)] + [_const_spec(a.shape) for a in weights],
        out_specs=[nat(SSM_WIDTH), nat(ATT_WIDTH), nat(ATT_WIDTH), nat(IDX_DIM), nat(ATT_WIDTH), nat(IDX_DIM),
                   nat(D_MODEL), nat(D_MODEL), tr(ATT_WIDTH), tr(ATT_WIDTH), tr(IDX_HEADS * IDX_DIM), tr(IDX_HEADS)],
        out_shape=[sds((bsz, s_len, SSM_WIDTH), F32), sds((bsz, s_len, ATT_WIDTH), F32),
                   sds((bsz, s_len, ATT_WIDTH), F32), sds((bsz, s_len, IDX_DIM), F32),
                   sds((bsz, s_len, ATT_WIDTH), BF16), sds((bsz, s_len, IDX_DIM), BF16),
                   sds((bsz, s_len, D_MODEL), BF16), sds((bsz, s_len, D_MODEL), BF16),
                   sds((bsz, ATT_WIDTH, s_len), BF16), sds((bsz, ATT_WIDTH, s_len), BF16),
                   sds((bsz, IDX_HEADS * IDX_DIM, s_len), BF16), sds((bsz, IDX_HEADS, s_len), F32)],
        compiler_params=_params(("parallel", "parallel")),
        name="in_proj_prompt",
    )(x, *weights)


def _in_proj_sample_kernel(x_ref, wu, wq, wk, wv, wqi, wki, wwi, wga, wgb,
                           u_o, q_o, k_o, v_o, qi_o, ki_o, wi_o, sga_o, sgb_o):
    xb = x_ref[...].astype(BF16)
    u_o[...] = _dot(xb, wu[...])
    q_o[...] = _dot(xb, wq[...])
    k_o[...] = _dot(xb, wk[...])
    v_o[...] = _dot(xb, wv[...])
    qi_o[...] = _dot(xb, wqi[...])
    ki_o[...] = _dot(xb, wki[...])
    wi_o[...] = _dot(xb, wwi[...])
    sga_o[...] = jax.nn.sigmoid(_dot(xb, wga[...])).astype(BF16)
    sgb_o[...] = jax.nn.sigmoid(_dot(xb, wgb[...])).astype(BF16)


def _in_proj_sample(x, w):
    n = x.shape[0]
    weights = [w['u'], w['q'], w['k'], w['v'], w['qi'], w['ki'], w['wi'], w['ga'], w['gb']]
    cols = [SSM_WIDTH, ATT_WIDTH, ATT_WIDTH, ATT_WIDTH, IDX_HEADS * IDX_DIM, IDX_DIM, IDX_HEADS, D_MODEL, D_MODEL]
    dts = [F32] * 7 + [BF16, BF16]
    return pl.pallas_call(
        _in_proj_sample_kernel,
        grid=(1,),
        in_specs=[_const_spec(x.shape)] + [_const_spec(a.shape) for a in weights],
        out_specs=[_const_spec((n, c)) for c in cols],
        out_shape=[jax.ShapeDtypeStruct((n, c), d) for c, d in zip(cols, dts)],
        compiler_params=_params(("arbitrary",)),
        name="in_proj_sample",
    )(x, *weights)


def _glu_tail(h_bf, u, cfull, dvec, wglu):
    y = _dot(h_bf, cfull[...]) + dvec[...] * u
    y = jax.nn.gelu(y)
    return y * jax.nn.sigmoid(_dot(y.astype(BF16), wglu[...]))


def _ssm_prompt_kernel(u_ref, perm_ref, permt_ref, bfull, cfull, lre_ref, lim_ref, dvec, wglu,
                       ya_o, hre_o, him_o, h_scr, u_scr, y_scr, state_scr, *, bsz, tt):
    c = pl.program_id(0)
    rows = bsz * tt
    row_chunk = 256

    @pl.when(c == 0)
    def _():
        state_scr[...] = jnp.zeros_like(state_scr)

    u_bt = jnp.concatenate([u_ref[b] for b in range(bsz)], axis=0)
    u_hi = u_bt.astype(BF16)
    u_lo = (u_bt - u_hi.astype(F32)).astype(BF16)
    u_scr[...] = _dot(perm_ref[...], u_hi) + _dot(perm_ref[...], u_lo)
    for r in range(rows // row_chunk):
        sl = slice(r * row_chunk, (r + 1) * row_chunk)
        h_scr[sl, :] = _dot(u_scr[sl, :].astype(BF16), bfull[...])

    chunk = 512
    for ci in range(SSM_COLS // chunk):
        re_sl = slice(ci * chunk, (ci + 1) * chunk)
        im_sl = slice(SSM_COLS + ci * chunk, SSM_COLS + (ci + 1) * chunk)
        lre = lre_ref[:, re_sl]
        lim = lim_ref[:, re_sl]

        def step(t, carry):
            hr, hi = carry
            r0 = pl.multiple_of(t * bsz, bsz)
            nr = lre * hr - lim * hi + h_scr[pl.ds(r0, bsz), re_sl]
            ni = lre * hi + lim * hr + h_scr[pl.ds(r0, bsz), im_sl]
            h_scr[pl.ds(r0, bsz), re_sl] = nr
            h_scr[pl.ds(r0, bsz), im_sl] = ni
            return nr, ni

        hr, hi = lax.fori_loop(0, tt, step, (state_scr[:, re_sl], state_scr[:, im_sl]), unroll=4)
        state_scr[:, re_sl] = hr
        state_scr[:, im_sl] = hi

    for r in range(rows // row_chunk):
        sl = slice(r * row_chunk, (r + 1) * row_chunk)
        y_scr[sl, :] = _glu_tail(h_scr[sl, :].astype(BF16), u_scr[sl, :], cfull, dvec, wglu).astype(BF16)

    ya_bt = _dot(permt_ref[...], y_scr[...]).astype(BF16)
    for b in range(bsz):
        ya_o[b] = ya_bt[b * tt:(b + 1) * tt]

    @pl.when(c == pl.num_programs(0) - 1)
    def _():
        hre_o[...] = state_scr[:, :SSM_COLS]
        him_o[...] = state_scr[:, SSM_COLS:]


def _ssm_prompt(u, ops, wglu):
    bsz, s_len, _ = u.shape
    tt = min(SSM_T_TILE, s_len)
    rows = bsz * tt
    lre = jnp.broadcast_to(ops['lre'], (bsz, SSM_COLS))
    lim = jnp.broadcast_to(ops['lim'], (bsz, SSM_COLS))
    kern = functools.partial(_ssm_prompt_kernel, bsz=bsz, tt=tt)
    src = (jnp.arange(rows, dtype=I32) % bsz) * tt + jnp.arange(rows, dtype=I32) // bsz
    perm = (src[:, None] == jnp.arange(rows, dtype=I32)[None, :]).astype(BF16)
    return pl.pallas_call(
        kern,
        grid=(s_len // tt,),
        in_specs=[pl.BlockSpec((bsz, tt, SSM_WIDTH), lambda c: (0, c, 0)),
                  _const_spec(perm.shape), _const_spec(perm.shape),
                  _const_spec(ops['bfull'].shape), _const_spec(ops['cfull'].shape),
                  _const_spec(lre.shape), _const_spec(lim.shape), _const_spec(ops['d'].shape),
                  _const_spec(wglu.shape)],
        out_specs=[pl.BlockSpec((bsz, tt, SSM_WIDTH), lambda c: (0, c, 0)),
                   _const_spec((bsz, SSM_COLS)), _const_spec((bsz, SSM_COLS))],
        out_shape=[jax.ShapeDtypeStruct((bsz, s_len, SSM_WIDTH), BF16),
                   jax.ShapeDtypeStruct((bsz, SSM_COLS), F32), jax.ShapeDtypeStruct((bsz, SSM_COLS), F32)],
        scratch_shapes=[pltpu.VMEM((rows, 2 * SSM_COLS), F32), pltpu.VMEM((rows, SSM_WIDTH), F32),
                        pltpu.VMEM((rows, SSM_WIDTH), BF16), pltpu.VMEM((bsz, 2 * SSM_COLS), F32)],
        compiler_params=_params(("arbitrary",)),
        name="ssm_prompt",
    )(u, perm, perm.T, ops['bfull'], ops['cfull'], lre, lim, ops['d'], wglu)


def _ssm_sample_kernel(u_ref, h0re_ref, h0im_ref, bfull, cfull, lre_ref, lim_ref, dvec, wglu,
                       ya_o, hre_o, him_o):
    u = u_ref[...]
    bu = _dot(u.astype(BF16), bfull[...])
    lre = lre_ref[...]
    lim = lim_ref[...]
    h0r = h0re_ref[...]
    h0i = h0im_ref[...]
    hr = lre * h0r - lim * h0i + bu[:, :SSM_COLS]
    hi = lre * h0i + lim * h0r + bu[:, SSM_COLS:]
    hre_o[...] = hr
    him_o[...] = hi
    h_bf = jnp.concatenate([hr.astype(BF16), hi.astype(BF16)], axis=1)
    ya_o[...] = _glu_tail(h_bf, u, cfull, dvec, wglu).astype(BF16)


def _ssm_sample(u, h0re, h0im, ops, wglu):
    n = u.shape[0]
    args = [u, h0re, h0im, ops['bfull'], ops['cfull'], ops['lre'], ops['lim'], ops['d'], wglu]
    return pl.pallas_call(
        _ssm_sample_kernel,
        grid=(1,),
        in_specs=[_const_spec(a.shape) for a in args],
        out_specs=[_const_spec((n, SSM_WIDTH)), _const_spec((n, SSM_COLS)), _const_spec((n, SSM_COLS))],
        out_shape=[jax.ShapeDtypeStruct((n, SSM_WIDTH), BF16),
                   jax.ShapeDtypeStruct((n, SSM_COLS), F32), jax.ShapeDtypeStruct((n, SSM_COLS), F32)],
        compiler_params=_params(("arbitrary",)),
        name="ssm_sample",
    )(*args)


def _bisect_kth(count_ge, lo, hi, c_lo, c_hi, k):
    def cond(st):
        it, lo, hi, _, _ = st
        return jnp.logical_and(it < 34, jnp.max((lo + 1 != hi).astype(I32)) > 0)

    def body(st):
        it, lo, hi, c_lo, c_hi = st
        mid = (lo >> 1) + (hi >> 1) + (lo & hi & 1)
        cnt = count_ge(mid)
        active = lo + 1 != hi
        up = jnp.logical_and(active, cnt >= k)
        down = jnp.logical_and(active, cnt < k)
        n_lo = jnp.where(up, mid, lo)
        n_clo = jnp.where(up, cnt, c_lo)
        n_hi = jnp.where(down, mid, jnp.where(jnp.logical_and(active, cnt == k), mid + 1, hi))
        n_chi = jnp.where(down, cnt, c_hi)
        return it + 1, n_lo, n_hi, n_clo, n_chi

    _, lo, hi, c_lo, c_hi = lax.while_loop(cond, body, (jnp.int32(0), lo, hi, c_lo, c_hi))
    return lo, c_lo, c_hi


def _bisect_tie_index(count_tie_le, need, n_tie_rows, idx_hi, n_bits):
    def cond(st):
        it, _, _ = st
        return jnp.logical_and(it < n_bits, n_tie_rows > 0)

    def body(st):
        it, lo, hi = st
        mid = (lo + hi) >> 1
        ok = count_tie_le(mid) >= need
        return it + 1, jnp.where(ok, lo, mid), jnp.where(ok, mid, hi)

    lo0 = jnp.full_like(need, -1)
    hi0 = jnp.full_like(need, idx_hi)
    _, _, hi = lax.while_loop(cond, body, (jnp.int32(0), lo0, hi0))
    return hi


def _attn_prompt_kernel(qt_ref, qit_ref, wit_ref, k_ref, vt_ref, ki_ref, bias_ref, yb_o, s_scr, o_scr, *, k_sel):
    i = pl.program_id(1)
    n_tiles = i + 1
    tq = Q_TILE
    tk = KV_TILE
    q_pos = i * tq + lax.broadcasted_iota(I32, (1, tq), 1)
    kv_iota = lax.broadcasted_iota(I32, (tk, tq), 0)

    def score_tile(j, _):
        r0 = pl.multiple_of(j * tk, tk)
        ki_t = ki_ref[0, pl.ds(r0, tk), :]
        acc = jnp.zeros((tk, tq), F32)
        for h in range(IDX_HEADS):
            s = _dot(ki_t, qit_ref[0, h * IDX_DIM:(h + 1) * IDX_DIM, :])
            acc = acc + wit_ref[0, h:h + 1, :] * jnp.maximum(s, 0.0)
        acc = jnp.where(kv_iota + r0 <= q_pos, acc, -jnp.inf)
        s_scr[pl.ds(r0, tk), :] = _ordered_key(acc)
        return 0

    lax.fori_loop(0, n_tiles, score_tile, 0)

    def count_tiles(pred):
        def tile(j, acc):
            r0 = pl.multiple_of(j * tk, tk)
            ind = pred(s_scr[pl.ds(r0, tk), :], r0).astype(I32)
            for r in range(tk // 8):
                acc = acc + ind[r * 8:(r + 1) * 8]
            return acc
        acc = lax.fori_loop(0, n_tiles, tile, jnp.zeros((8, tq), I32))
        return jnp.sum(acc, axis=0, keepdims=True)

    n_valid = q_pos + 1
    searching = n_valid > k_sel
    lo0 = jnp.full((1, tq), KEY_NEG_INF + 1, I32)
    hi0 = jnp.where(searching, KEY_MAX, lo0 + 1)
    thr, c_thr, c_above = _bisect_kth(lambda mid: count_tiles(lambda t, r0: t >= mid),
                                      lo0, hi0, n_valid, jnp.zeros((1, tq), I32), k_sel)
    has_tie = jnp.logical_and(searching, c_thr > k_sel)
    need = k_sel - c_above
    tie_j = _bisect_tie_index(
        lambda mid: count_tiles(lambda t, r0: jnp.logical_and(t == thr, kv_iota + r0 <= mid)),
        need, jnp.max(has_tie.astype(I32)), n_tiles * tk - 1, 14)
    tie_j = jnp.where(has_tie, tie_j, KEY_MAX)

    def mask_tile(j, _):
        r0 = pl.multiple_of(j * tk, tk)
        t = s_scr[pl.ds(r0, tk), :]
        tie_ok = jnp.where(kv_iota + r0 <= tie_j, 0.0, NEG_BIG)
        madd = jnp.where(t > thr, 0.0, jnp.where(t == thr, tie_ok, NEG_BIG))
        s_scr[pl.ds(r0, tk), :] = pltpu.bitcast(madd.astype(F32), I32)
        return 0

    lax.fori_loop(0, n_tiles, mask_tile, 0)

    row = lax.broadcasted_iota(I32, (2 * HEAD_DIM, tq), 0)
    for pair in range(N_HEADS // 2):
        q_pair = qt_ref[0, pair * 128:(pair + 1) * 128, :]
        zero = jnp.zeros_like(q_pair)
        q_heads = (jnp.where(row < HEAD_DIM, q_pair, zero), jnp.where(row >= HEAD_DIM, q_pair, zero))

        def attn_tile(j, carry):
            r0 = pl.multiple_of(j * tk, tk)
            k_t = k_ref[0, pl.ds(r0, tk), pair * 128:(pair + 1) * 128]
            madd = pltpu.bitcast(s_scr[pl.ds(r0, tk), :], F32)
            kind = jnp.minimum(i - j, 2)
            out = []
            for hh in range(2):
                h = 2 * pair + hh
                m, l, acc = carry[hh]
                x = _dot(k_t, q_heads[hh]) + madd + bias_ref[h, kind]
                m_new = jnp.maximum(m, jnp.max(x, axis=0, keepdims=True))
                alpha = jnp.exp(m - m_new)
                p = jnp.exp(x - m_new)
                l = alpha * l + jnp.sum(p, axis=0, keepdims=True)
                v_t = vt_ref[0, h * HEAD_DIM:(h + 1) * HEAD_DIM, pl.ds(r0, tk)]
                acc = alpha * acc + _dot(v_t, p.astype(BF16))
                out.append((m_new, l, acc))
            return tuple(out)

        init = tuple((jnp.full((1, tq), NEG_BIG, F32), jnp.zeros((1, tq), F32), jnp.zeros((HEAD_DIM, tq), F32))
                     for _ in range(2))
        res = lax.fori_loop(0, n_tiles, attn_tile, init)
        for hh in range(2):
            h = 2 * pair + hh
            _, l, acc = res[hh]
            o_scr[h * HEAD_DIM:(h + 1) * HEAD_DIM, :] = acc / l

    yb_o[0] = o_scr[...].T.astype(BF16)


def _attn_prompt(qt, qit, wit, k_bf, vt, ki_bf, bias_t):
    bsz, _, s_len = qt.shape
    assert s_len % Q_TILE == 0 and Q_TILE == KV_TILE
    qblk = lambda c: pl.BlockSpec((1, c, Q_TILE), lambda b, i: (b, 0, i))
    return pl.pallas_call(
        functools.partial(_attn_prompt_kernel, k_sel=min(TOPK_MAX, s_len // 4)),
        grid=(bsz, s_len // Q_TILE),
        in_specs=[qblk(ATT_WIDTH), qblk(IDX_HEADS * IDX_DIM), qblk(IDX_HEADS),
                  _resident_spec((1, s_len, ATT_WIDTH), lambda b, i: (b, 0, 0)),
                  _resident_spec((1, ATT_WIDTH, s_len), lambda b, i: (b, 0, 0)),
                  _resident_spec((1, s_len, IDX_DIM), lambda b, i: (b, 0, 0)),
                  _resident_spec(bias_t.shape, lambda b, i: (0, 0, 0, 0))],
        out_specs=pl.BlockSpec((1, Q_TILE, ATT_WIDTH), lambda b, i: (b, i, 0)),
        out_shape=jax.ShapeDtypeStruct((bsz, s_len, ATT_WIDTH), BF16),
        scratch_shapes=[pltpu.VMEM((s_len, Q_TILE), I32), pltpu.VMEM((ATT_WIDTH, Q_TILE), F32)],
        compiler_params=_params(("parallel", "arbitrary")),
        name="attn_prompt",
    )(qt, qit, wit, k_bf, vt, ki_bf, bias_t)


def _attn_sample_select_kernel(pt_ref, qi_ref, wi_ref, kin_ref, *rest, n_pages):
    gp = SAMPLE_IDX_PAGES
    pages = rest[:gp]
    madd_o, newm_o, s_scr = rest[gp:]
    c = pl.program_id(1)
    n_chunks = n_pages // gp
    qi = qi_ref[0]
    wi = wi_ref[0]
    rows = []
    for g in range(gp):
        kb = pages[g][0].astype(BF16)
        s = _dot_nt(qi, kb)
        rows.append(jnp.sum(wi * jnp.maximum(s, 0.0), axis=0, keepdims=True))
    keys = _ordered_key(jnp.concatenate(rows, axis=0))
    s_scr[pl.ds(pl.multiple_of(c * gp, gp), gp), :] = keys

    @pl.when(c == n_chunks - 1)
    def _():
        past = n_pages * PAGE_SIZE
        s_new = jnp.sum(qi.astype(F32) * kin_ref[0].astype(F32), axis=1, keepdims=True)
        key_new = _ordered_key(jnp.sum(wi * jnp.maximum(s_new, 0.0), axis=0, keepdims=True))
        idx = (lax.broadcasted_iota(I32, (n_pages, PAGE_SIZE), 0) * PAGE_SIZE
               + lax.broadcasted_iota(I32, (n_pages, PAGE_SIZE), 1))

        def total(ind_past, ind_new):
            a = jnp.sum(ind_past.astype(I32), axis=0, keepdims=True)
            return jnp.sum(a, axis=1, keepdims=True) + ind_new.astype(I32)

        k_sel = min(TOPK_MAX, (past + 1) // 4)
        n_valid = jnp.full((1, 1), past + 1, I32)
        lo0 = jnp.full((1, 1), KEY_NEG_INF + 1, I32)
        hi0 = jnp.full((1, 1), KEY_MAX, I32)
        thr, c_thr, c_above = _bisect_kth(lambda mid: total(s_scr[...] >= mid, key_new >= mid),
                                          lo0, hi0, n_valid, jnp.zeros((1, 1), I32), k_sel)
        has_tie = c_thr > k_sel
        need = k_sel - c_above
        tie_j = _bisect_tie_index(
            lambda mid: total(jnp.logical_and(s_scr[...] == thr, idx <= mid),
                              jnp.logical_and(key_new == thr, past <= mid)),
            need, jnp.max(has_tie.astype(I32)), past, 14)
        tie_j = jnp.where(has_tie, tie_j, KEY_MAX)
        t = s_scr[...]
        tie_ok = jnp.where(idx <= tie_j, 0.0, NEG_BIG)
        madd_o[0] = jnp.where(t > thr, 0.0, jnp.where(t == thr, tie_ok, NEG_BIG)).astype(F32)
        new_ok = jnp.where(past <= tie_j, 0.0, NEG_BIG)
        newm = jnp.where(key_new > thr, 0.0, jnp.where(key_new == thr, new_ok, NEG_BIG)).astype(F32)
        newm_o[0] = jnp.broadcast_to(newm, (1, PAGE_SIZE))


def _attn_sample_select(page_table, qi, wi, ki_new, cache_idx_k):
    n, n_pages = page_table.shape
    gp = SAMPLE_IDX_PAGES
    assert n_pages % gp == 0
    page_spec = lambda g: pl.BlockSpec((1, PAGE_SIZE, IDX_DIM), lambda b, c, pt: (pt[b, c * gp + g], 0, 0))
    per_seq = lambda shape: pl.BlockSpec((1,) + shape, lambda b, c, pt: (b, 0, 0))
    grid_spec = pltpu.PrefetchScalarGridSpec(
        num_scalar_prefetch=1,
        grid=(n, n_pages // gp),
        in_specs=[per_seq((IDX_HEADS, IDX_DIM)), per_seq((IDX_HEADS, 1)), per_seq((1, IDX_DIM))]
        + [page_spec(g) for g in range(gp)],
        out_specs=[per_seq((n_pages, PAGE_SIZE)), per_seq((1, PAGE_SIZE))],
        scratch_shapes=[pltpu.VMEM((n_pages, PAGE_SIZE), I32)],
    )
    return pl.pallas_call(
        functools.partial(_attn_sample_select_kernel, n_pages=n_pages),
        grid_spec=grid_spec,
        out_shape=[jax.ShapeDtypeStruct((n, n_pages, PAGE_SIZE), F32), jax.ShapeDtypeStruct((n, 1, PAGE_SIZE), F32)],
        compiler_params=_params(("parallel", "arbitrary")),
        name="attn_sample_select",
    )(page_table, qi, wi, ki_new, *([cache_idx_k] * gp))


def _attn_sample_kernel(pt_ref, q_ref, kn_ref, vn_ref, madd_ref, newm_ref, bias_ref, biasn_ref, *rest, n_pages):
    gp = SAMPLE_KV_PAGES
    k_pages = rest[:gp]
    v_pages = rest[gp:2 * gp]
    yb_o, m_scr, l_scr, acc_scr = rest[2 * gp:]
    c = pl.program_id(1)
    n_chunks = n_pages // gp
    head_of_lane = lax.broadcasted_iota(I32, (N_HEADS, ATT_WIDTH), 1) // HEAD_DIM
    head_of_row = lax.broadcasted_iota(I32, (N_HEADS, ATT_WIDTH), 0)
    own = head_of_lane == head_of_row
    q_blk = jnp.where(own, jnp.broadcast_to(q_ref[0], (N_HEADS, ATT_WIDTH)), 0.0)

    @pl.when(c == 0)
    def _():
        kn = kn_ref[0].astype(BF16).astype(F32)
        x_new = (jnp.sum(q_blk.astype(BF16).astype(F32) * kn, axis=1, keepdims=True)
                 + biasn_ref[...] + newm_ref[0][:, :1])
        m_scr[...] = x_new
        l_scr[...] = jnp.ones_like(l_scr)
        acc_scr[...] = jnp.broadcast_to(vn_ref[0].astype(BF16).astype(F32), (N_HEADS, ATT_WIDTH))

    q_bf = q_blk.astype(BF16)
    xs = []
    for g in range(gp):
        pg = c * gp + g
        kb = k_pages[g][0].astype(BF16)
        xs.append(_dot_nt(q_bf, kb) + madd_ref[0, pg] + bias_ref[pg])
    x = jnp.concatenate(xs, axis=1)
    m = m_scr[...]
    m_new = jnp.maximum(m, jnp.max(x, axis=1, keepdims=True))
    alpha = jnp.exp(m - m_new)
    p = jnp.exp(x - m_new)
    l_scr[...] = alpha * l_scr[...] + jnp.sum(p, axis=1, keepdims=True)
    p_bf = p.astype(BF16)
    pv = jnp.zeros((N_HEADS, ATT_WIDTH), F32)
    for g in range(gp):
        pv = pv + _dot(p_bf[:, g * PAGE_SIZE:(g + 1) * PAGE_SIZE], v_pages[g][0].astype(BF16))
    acc_scr[...] = alpha * acc_scr[...] + pv
    m_scr[...] = m_new

    @pl.when(c == n_chunks - 1)
    def _():
        o = acc_scr[...] / l_scr[...]
        yb_o[0] = jnp.sum(jnp.where(own, o, 0.0), axis=0, keepdims=True).astype(BF16)


def _attn_sample(page_table, q, k_new, v_new, madd, newm, bias_pages, bias_new, cache_k, cache_v):
    n, n_pages = page_table.shape
    gp = SAMPLE_KV_PAGES
    assert n_pages % gp == 0
    page_spec = lambda g: pl.BlockSpec((1, PAGE_SIZE, ATT_WIDTH), lambda b, c, pt: (pt[b, c * gp + g], 0, 0))
    per_seq = lambda shape: pl.BlockSpec((1,) + shape, lambda b, c, pt: (b, 0, 0))
    grid_spec = pltpu.PrefetchScalarGridSpec(
        num_scalar_prefetch=1,
        grid=(n, n_pages // gp),
        in_specs=[per_seq((1, ATT_WIDTH)), per_seq((1, ATT_WIDTH)), per_seq((1, ATT_WIDTH)),
                  pl.BlockSpec((1, n_pages, 1, PAGE_SIZE), lambda b, c, pt: (b, 0, 0, 0)), per_seq((1, PAGE_SIZE)),
                  pl.BlockSpec(bias_pages.shape, lambda b, c, pt: (0, 0, 0)),
                  pl.BlockSpec(bias_new.shape, lambda b, c, pt: (0, 0))]
        + [page_spec(g) for g in range(gp)] * 2,
        out_specs=per_seq((1, ATT_WIDTH)),
        scratch_shapes=[pltpu.VMEM((N_HEADS, 1), F32), pltpu.VMEM((N_HEADS, 1), F32),
                        pltpu.VMEM((N_HEADS, ATT_WIDTH), F32)],
    )
    return pl.pallas_call(
        functools.partial(_attn_sample_kernel, n_pages=n_pages),
        grid_spec=grid_spec,
        out_shape=jax.ShapeDtypeStruct((n, 1, ATT_WIDTH), BF16),
        compiler_params=_params(("parallel", "arbitrary")),
        name="attn_sample",
    )(page_table, q, k_new, v_new, madd, newm, bias_pages, bias_new, *([cache_k] * gp), *([cache_v] * gp))


def _merge_kernel(x_ref, ya_ref, yb_ref, sga_ref, sgb_ref, wa, wb, wo, g_ref, b_ref, x1_o, *, alpha):
    merged = (sga_ref[...].astype(F32) * _dot(ya_ref[...], wa[...])
              + sgb_ref[...].astype(F32) * _dot(yb_ref[...], wb[...]))
    z = alpha * x_ref[...] + _dot(merged.astype(BF16), wo[...])
    x1_o[...] = _layer_norm(z, g_ref[...], b_ref[...])


def _merge(x, ya, yb, sga, sgb, wa, wb, wo, g, b, alpha):
    n = x.shape[0]
    tm = min(TOKEN_TILE, n)
    rows = lambda c: pl.BlockSpec((tm, c), lambda i: (i, 0))
    return pl.pallas_call(
        functools.partial(_merge_kernel, alpha=alpha),
        grid=(n // tm,),
        in_specs=[rows(D_MODEL), rows(SSM_WIDTH), rows(ATT_WIDTH), rows(D_MODEL), rows(D_MODEL),
                  _const_spec(wa.shape), _const_spec(wb.shape), _const_spec(wo.shape),
                  _const_spec(g.shape), _const_spec(b.shape)],
        out_specs=rows(D_MODEL),
        out_shape=jax.ShapeDtypeStruct((n, D_MODEL), F32),
        compiler_params=_params(("parallel",)),
        name="merge_out_proj",
    )(x, ya, yb, sga, sgb, wa, wb, wo, g, b)


def _ffn_kernel(x1_ref, wg, wu, wd, g_ref, b_ref, y_o, acc_scr, *, alpha):
    x1 = x1_ref[...]
    xb = x1.astype(BF16)
    acc_scr[...] = alpha * x1
    for c in range(D_FF // FF_CHUNK):
        sl = slice(c * FF_CHUNK, (c + 1) * FF_CHUNK)
        gate = _dot(xb, wg[:, sl])
        up = _dot(xb, wu[:, sl])
        hid = (jax.nn.silu(gate) * up).astype(BF16)
        acc_scr[...] += _dot(hid, wd[sl, :])
    y_o[...] = _layer_norm(acc_scr[...], g_ref[...], b_ref[...])


def _ffn(x1, wg, wu, wd, g, b, alpha):
    n = x1.shape[0]
    tm = min(TOKEN_TILE, n)
    rows = pl.BlockSpec((tm, D_MODEL), lambda i: (i, 0))
    return pl.pallas_call(
        functools.partial(_ffn_kernel, alpha=alpha),
        grid=(n // tm,),
        in_specs=[rows, _const_spec(wg.shape), _const_spec(wu.shape), _const_spec(wd.shape),
                  _const_spec(g.shape), _const_spec(b.shape)],
        out_specs=rows,
        out_shape=jax.ShapeDtypeStruct((n, D_MODEL), F32),
        scratch_shapes=[pltpu.VMEM((tm, D_MODEL), F32)],
        compiler_params=_params(("parallel",)),
        name="ffn",
    )(x1, wg, wu, wd, g, b)


def _rel_bucket(rel):
    n = jnp.maximum(rel, 0)
    max_exact = REL_BUCKETS // 2
    nf = jnp.maximum(n, max_exact).astype(F32)
    large = max_exact + (jnp.log(nf / max_exact) / math.log(REL_MAX_DIST / max_exact)
                         * (REL_BUCKETS - max_exact)).astype(I32)
    large = jnp.minimum(large, REL_BUCKETS - 1)
    return jnp.where(n < max_exact, n, large)


def _prompt_bias_tiles(rel_bias):
    kl = jnp.arange(KV_TILE, dtype=I32)[:, None]
    ql = jnp.arange(Q_TILE, dtype=I32)[None, :]
    rel = jnp.stack([ql - kl, ql - kl + KV_TILE, jnp.full((KV_TILE, Q_TILE), 2 * KV_TILE, I32)])
    assert KV_TILE + 1 >= REL_MAX_DIST
    return jnp.moveaxis(rel_bias.astype(F32)[_rel_bucket(rel)], -1, 0)


def _ssm_operators(a_re, a_im, log_dt, b_re, b_im, c_re, c_im, d):
    g, p, gi = SSM_GROUPS, SSM_STATE, SSM_GROUP
    a_re, a_im = a_re.astype(F32), a_im.astype(F32)
    dt = jnp.exp(log_dt.astype(F32))[:, None]
    mag = jnp.exp(dt * a_re)
    lre = mag * jnp.cos(dt * a_im)
    lim = mag * jnp.sin(dt * a_im)
    den = a_re * a_re + a_im * a_im
    fr = ((lre - 1.0) * a_re + lim * a_im) / den
    fi = (lim * a_re - (lre - 1.0) * a_im) / den
    bbr = fr[:, :, None] * b_re - fi[:, :, None] * b_im
    bbi = fr[:, :, None] * b_im + fi[:, :, None] * b_re
    eye = jnp.eye(g, dtype=F32)
    blockdiag_in = lambda m: jnp.einsum('gpi,gh->gihp', m, eye).reshape(g * gi, g * p)
    blockdiag_out = lambda m: jnp.einsum('gip,gh->gphi', m, eye).reshape(g * p, g * gi)
    bfull = jnp.concatenate([blockdiag_in(bbr), blockdiag_in(bbi)], axis=1)
    cfull = jnp.concatenate([blockdiag_out(c_re.astype(F32)), blockdiag_out(-c_im.astype(F32))], axis=0)
    return {'lre': lre.reshape(1, g * p), 'lim': lim.reshape(1, g * p),
            'bfull': bfull.astype(BF16), 'cfull': cfull.astype(BF16), 'd': d.astype(F32).reshape(1, g * gi)}


def _split_w_in(w_in):
    offs = [0]
    for n in IN_SPLITS:
        offs.append(offs[-1] + n)
    names = ['u', 'q', 'k', 'v', 'qi', 'ki', 'wi', 'ga', 'gb']
    return {nm: w_in[:, offs[j]:offs[j + 1]] for j, nm in enumerate(names)}


def kernel(x_prompt, x_sample, cache_k, cache_v, cache_idx_k, state_ssm_re, state_ssm_im, page_table, w_in, ssm_a_re, ssm_a_im, ssm_log_dt, ssm_b_re, ssm_b_im, ssm_c_re, ssm_c_im, ssm_d, w_glu, w_branch_a, w_branch_b, rel_bias, w_out, ln1_g, ln1_b, w_ffn_in, w_ffn_out, ln2_g, ln2_b):
    depth = w_in.shape[0]
    bsz, s_len, _ = x_prompt.shape
    n_dec, dec_seq, _ = x_sample.shape
    assert dec_seq == 1
    n_pages = page_table.shape[1]
    past = n_pages * PAGE_SIZE
    alpha = (2 * depth) ** 0.25
    idx_scale = IDX_DIM ** -0.5 * IDX_HEADS ** -0.5
    att_scale = HEAD_DIM ** -0.5

    bias_t = _prompt_bias_tiles(rel_bias)
    key_pos = jnp.arange(past, dtype=I32).reshape(n_pages, 1, PAGE_SIZE)
    bias_pages = jnp.moveaxis(rel_bias.astype(F32)[_rel_bucket(past - key_pos)], -1, 2)[:, 0]
    bias_new = rel_bias.astype(F32)[_rel_bucket(jnp.zeros((), I32))].reshape(N_HEADS, 1)

    x_p = x_prompt
    x_s = x_sample.reshape(n_dec, D_MODEL)
    outs = {k: [] for k in ('kp', 'vp', 'kip', 'hpr', 'hpi', 'ks', 'vs', 'kis', 'hsr', 'hsi')}
    for l in range(depth):
        w = _split_w_in(w_in[l])
        wn = {nm: w[nm].astype(BF16) for nm in ('u', 'k', 'v', 'ki', 'ga', 'gb')}
        wn['tq'] = (w['q'] * att_scale).T.astype(BF16)
        wn['tv'] = w['v'].T.astype(BF16)
        wn['tqi'] = w['qi'].T.astype(BF16)
        wn['twi'] = jnp.pad((w['wi'] * idx_scale).T, ((0, 16 - IDX_HEADS), (0, 0))).astype(BF16)
        ws = {nm: w[nm].astype(BF16) for nm in ('u', 'k', 'v', 'qi', 'ki', 'ga', 'gb')}
        ws['q'] = (w['q'] * att_scale).astype(BF16)
        ws['wi'] = (w['wi'] * idx_scale).astype(BF16)
        ops = _ssm_operators(ssm_a_re[l], ssm_a_im[l], ssm_log_dt[l], ssm_b_re[l], ssm_b_im[l],
                             ssm_c_re[l], ssm_c_im[l], ssm_d[l])
        wglu = w_glu[l].astype(BF16)
        wa, wb, wo = w_branch_a[l].astype(BF16), w_branch_b[l].astype(BF16), w_out[l].astype(BF16)
        wg, wu = w_ffn_in[l][:, :D_FF].astype(BF16), w_ffn_in[l][:, D_FF:].astype(BF16)
        wd = w_ffn_out[l].astype(BF16)
        g1, b1 = ln1_g[l].astype(F32).reshape(1, D_MODEL), ln1_b[l].astype(F32).reshape(1, D_MODEL)
        g2, b2 = ln2_g[l].astype(F32).reshape(1, D_MODEL), ln2_b[l].astype(F32).reshape(1, D_MODEL)

        u, k, v, ki, k_bf, ki_bf, sga, sgb, qt, vt, qit, wit = _in_proj_prompt(x_p, wn)
        ya, hpr, hpi = _ssm_prompt(u, ops, wglu)
        yb = _attn_prompt(qt, qit, wit, k_bf, vt, ki_bf, bias_t)
        n_tok = bsz * s_len
        flat = lambda a: a.reshape(n_tok, a.shape[-1])
        x1 = _merge(flat(x_p), flat(ya), flat(yb), flat(sga), flat(sgb), wa, wb, wo, g1, b1, alpha)
        x_p = _ffn(x1, wg, wu, wd, g2, b2, alpha).reshape(bsz, s_len, D_MODEL)
        outs['kp'].append(k.reshape(bsz, s_len, N_HEADS, HEAD_DIM))
        outs['vp'].append(v.reshape(bsz, s_len, N_HEADS, HEAD_DIM))
        outs['kip'].append(ki)
        outs['hpr'].append(hpr.reshape(bsz, SSM_GROUPS, SSM_STATE))
        outs['hpi'].append(hpi.reshape(bsz, SSM_GROUPS, SSM_STATE))

        us, qs, k_s, v_s, qis, kis, wis, sga_s, sgb_s = _in_proj_sample(x_s, ws)
        ya_s, hsr, hsi = _ssm_sample(us, state_ssm_re[l].reshape(n_dec, SSM_COLS).astype(F32),
                                     state_ssm_im[l].reshape(n_dec, SSM_COLS).astype(F32), ops, wglu)
        n_pool = cache_k.shape[1]
        madd, newm = _attn_sample_select(
            page_table, qis.astype(BF16).reshape(n_dec, IDX_HEADS, IDX_DIM), wis.reshape(n_dec, IDX_HEADS, 1),
            kis.astype(BF16).reshape(n_dec, 1, IDX_DIM), cache_idx_k[l])
        yb_s = _attn_sample(page_table, qs.reshape(n_dec, 1, ATT_WIDTH), k_s.reshape(n_dec, 1, ATT_WIDTH),
                            v_s.reshape(n_dec, 1, ATT_WIDTH), madd.reshape(n_dec, n_pages, 1, PAGE_SIZE), newm,
                            bias_pages, bias_new,
                            cache_k[l].reshape(n_pool, PAGE_SIZE, ATT_WIDTH),
                            cache_v[l].reshape(n_pool, PAGE_SIZE, ATT_WIDTH))
        x1_s = _merge(x_s, ya_s, yb_s.reshape(n_dec, ATT_WIDTH), sga_s, sgb_s, wa, wb, wo, g1, b1, alpha)
        x_s = _ffn(x1_s, wg, wu, wd, g2, b2, alpha)
        outs['ks'].append(k_s.reshape(n_dec, 1, N_HEADS, HEAD_DIM))
        outs['vs'].append(v_s.reshape(n_dec, 1, N_HEADS, HEAD_DIM))
        outs['kis'].append(kis.reshape(n_dec, 1, IDX_DIM))
        outs['hsr'].append(hsr.reshape(n_dec, SSM_GROUPS, SSM_STATE))
        outs['hsi'].append(hsi.reshape(n_dec, SSM_GROUPS, SSM_STATE))

    st = lambda name: jnp.stack(outs[name])
    return (x_p, x_s.reshape(n_dec, 1, D_MODEL),
            st('kp'), st('vp'), st('kip'), st('hpr'), st('hpi'),
            st('ks'), st('vs'), st('kis'), st('hsr'), st('hsi'))
```

```python
import functools
import math

import jax
import jax.numpy as jnp
from jax import lax
from jax.experimental import pallas as pl
from jax.experimental.pallas import tpu as pltpu

F32 = jnp.float32
BF16 = jnp.bfloat16
I32 = jnp.int32

D_MODEL = 1024
SSM_WIDTH = 512
SSM_GROUP = 16
SSM_GROUPS = 32
SSM_STATE = 64
SSM_COLS = SSM_GROUPS * SSM_STATE
N_HEADS = 8
HEAD_DIM = 64
ATT_WIDTH = 512
IDX_HEADS = 8
IDX_DIM = 32
TOPK_MAX = 256
PAGE_SIZE = 128
REL_BUCKETS = 32
REL_MAX_DIST = 128
D_FF = 2816
LN_EPS = 1e-5
IN_SPLITS = (SSM_WIDTH, ATT_WIDTH, ATT_WIDTH, ATT_WIDTH, IDX_HEADS * IDX_DIM, IDX_DIM, IDX_HEADS, D_MODEL, D_MODEL)
LOG2E = math.log2(math.e)

VMEM_LIMIT_BYTES = 56 * 1024 * 1024
NEG_BIG = -1e30
KEY_NEG_INF = -2139095041
KEY_MAX = 2147483647

TOKEN_TILE = 512
Q_TILE = 256
KV_TILE = 256
SSM_T_TILE = 64
FF_CHUNK = 256
SAMPLE_IDX_PAGES = 32
SAMPLE_KV_PAGES = 8


def _params(sem):
    return pltpu.CompilerParams(dimension_semantics=sem, vmem_limit_bytes=VMEM_LIMIT_BYTES)


def _const_spec(shape):
    n = len(shape)
    return pl.BlockSpec(shape, lambda *_: (0,) * n)


def _resident_spec(block, index_map):
    return pl.BlockSpec(block, index_map, pipeline_mode=pl.Buffered(1))


def _dot(a, b):
    return jnp.dot(a, b, preferred_element_type=F32)


def _dot_nt(a, b):
    return lax.dot_general(a, b, (((1,), (1,)), ((), ())), preferred_element_type=F32)


def _layer_norm(z, g, b):
    mu = jnp.mean(z, axis=-1, keepdims=True)
    zc = z - mu
    var = jnp.mean(zc * zc, axis=-1, keepdims=True)
    return zc * lax.rsqrt(var + LN_EPS) * g + b


def _ordered_key(x):
    b = pltpu.bitcast(x, I32)
    return b ^ ((b >> 31) & KEY_MAX)


def _in_proj_prompt_kernel(x_ref, wu, wk, wv, wga, wgb, wki, wtq, wtv, wtqi, wtwi,
                           u_o, k_o, v_o, ki_o, kbf_o, kibf_o, sga_o, sgb_o, qt_o, vt_o, qit_o, wit_o):
    xb = x_ref[0].astype(BF16)
    u_o[0] = _dot(xb, wu[...])
    k = _dot(xb, wk[...])
    k_o[0] = k
    kbf_o[0] = k.astype(BF16)
    v_o[0] = _dot(xb, wv[...])
    ki = _dot(xb, wki[...])
    ki_o[0] = ki
    kibf_o[0] = ki.astype(BF16)
    sga_o[0] = jax.nn.sigmoid(_dot(xb, wga[...])).astype(BF16)
    sgb_o[0] = jax.nn.sigmoid(_dot(xb, wgb[...])).astype(BF16)
    qt_o[0] = _dot_nt(wtq[...], xb).astype(BF16)
    vt_o[0] = _dot_nt(wtv[...], xb).astype(BF16)
    qit_o[0] = _dot_nt(wtqi[...], xb).astype(BF16)
    wit_o[0] = _dot_nt(wtwi[...], xb)[:IDX_HEADS]


def _in_proj_prompt(x, w):
    bsz, s_len, _ = x.shape
    tm = min(TOKEN_TILE, s_len)
    nat = lambda c: pl.BlockSpec((1, tm, c), lambda b, i: (b, i, 0))
    tr = lambda c: pl.BlockSpec((1, c, tm), lambda b, i: (b, 0, i))
    sds = jax.ShapeDtypeStruct
    weights = [w['u'], w['k'], w['v'], w['ga'], w['gb'], w['ki'], w['tq'], w['tv'], w['tqi'], w['twi']]
    return pl.pallas_call(
        _in_proj_prompt_kernel,
        grid=(bsz, s_len // tm),
        in_specs=[nat(D_MODEL)] + [_const_spec(a.shape) for a in weights],
        out_specs=[nat(SSM_WIDTH), nat(ATT_WIDTH), nat(ATT_WIDTH), nat(IDX_DIM), nat(ATT_WIDTH), nat(IDX_DIM),
                   nat(D_MODEL), nat(D_MODEL), tr(ATT_WIDTH), tr(ATT_WIDTH), tr(IDX_HEADS * IDX_DIM), tr(IDX_HEADS)],
        out_shape=[sds((bsz, s_len, SSM_WIDTH), F32), sds((bsz, s_len, ATT_WIDTH), F32),
                   sds((bsz, s_len, ATT_WIDTH), F32), sds((bsz, s_len, IDX_DIM), F32),
                   sds((bsz, s_len, ATT_WIDTH), BF16), sds((bsz, s_len, IDX_DIM), BF16),
                   sds((bsz, s_len, D_MODEL), BF16), sds((bsz, s_len, D_MODEL), BF16),
                   sds((bsz, ATT_WIDTH, s_len), BF16), sds((bsz, ATT_WIDTH, s_len), BF16),
                   sds((bsz, IDX_HEADS * IDX_DIM, s_len), BF16), sds((bsz, IDX_HEADS, s_len), F32)],
        compiler_params=_params(("parallel", "parallel")),
        name="in_proj_prompt",
    )(x, *weights)


def _in_proj_sample_kernel(x_ref, wu, wq, wk, wv, wqi, wki, wwi, wga, wgb,
                           u_o, q_o, k_o, v_o, qi_o, ki_o, wi_o, sga_o, sgb_o):
    xb = x_ref[...].astype(BF16)
    u_o[...] = _dot(xb, wu[...])
    q_o[...] = _dot(xb, wq[...])
    k_o[...] = _dot(xb, wk[...])
    v_o[...] = _dot(xb, wv[...])
    qi_o[...] = _dot(xb, wqi[...])
    ki_o[...] = _dot(xb, wki[...])
    wi_o[...] = _dot(xb, wwi[...])
    sga_o[...] = jax.nn.sigmoid(_dot(xb, wga[...])).astype(BF16)
    sgb_o[...] = jax.nn.sigmoid(_dot(xb, wgb[...])).astype(BF16)


def _in_proj_sample(x, w):
    n = x.shape[0]
    weights = [w['u'], w['q'], w['k'], w['v'], w['qi'], w['ki'], w['wi'], w['ga'], w['gb']]
    cols = [SSM_WIDTH, ATT_WIDTH, ATT_WIDTH, ATT_WIDTH, IDX_HEADS * IDX_DIM, IDX_DIM, IDX_HEADS, D_MODEL, D_MODEL]
    dts = [F32] * 7 + [BF16, BF16]
    return pl.pallas_call(
        _in_proj_sample_kernel,
        grid=(1,),
        in_specs=[_const_spec(x.shape)] + [_const_spec(a.shape) for a in weights],
        out_specs=[_const_spec((n, c)) for c in cols],
        out_shape=[jax.ShapeDtypeStruct((n, c), d) for c, d in zip(cols, dts)],
        compiler_params=_params(("arbitrary",)),
        name="in_proj_sample",
    )(x, *weights)


def _glu_tail(h_bf, u, cfull, dvec, wglu):
    y = _dot(h_bf, cfull[...]) + dvec[...] * u
    y = jax.nn.gelu(y)
    return y * jax.nn.sigmoid(_dot(y.astype(BF16), wglu[...]))


def _ssm_prompt_kernel(u_ref, perm_ref, permt_ref, bfull, cfull, lre_ref, lim_ref, dvec, wglu,
                       ya_o, hre_o, him_o, h_scr, u_scr, y_scr, state_scr, *, bsz, tt):
    c = pl.program_id(0)
    rows = bsz * tt
    row_chunk = 256

    @pl.when(c == 0)
    def _():
        state_scr[...] = jnp.zeros_like(state_scr)

    u_bt = jnp.concatenate([u_ref[b] for b in range(bsz)], axis=0)
    u_hi = u_bt.astype(BF16)
    u_lo = (u_bt - u_hi.astype(F32)).astype(BF16)
    u_scr[...] = _dot(perm_ref[...], u_hi) + _dot(perm_ref[...], u_lo)
    for r in range(rows // row_chunk):
        sl = slice(r * row_chunk, (r + 1) * row_chunk)
        h_scr[sl, :] = _dot(u_scr[sl, :].astype(BF16), bfull[...])

    chunk = 512
    for ci in range(SSM_COLS // chunk):
        re_sl = slice(ci * chunk, (ci + 1) * chunk)
        im_sl = slice(SSM_COLS + ci * chunk, SSM_COLS + (ci + 1) * chunk)
        lre = lre_ref[:, re_sl]
        lim = lim_ref[:, re_sl]

        def step(t, carry):
            hr, hi = carry
            r0 = pl.multiple_of(t * bsz, bsz)
            nr = lre * hr - lim * hi + h_scr[pl.ds(r0, bsz), re_sl]
            ni = lre * hi + lim * hr + h_scr[pl.ds(r0, bsz), im_sl]
            h_scr[pl.ds(r0, bsz), re_sl] = nr
            h_scr[pl.ds(r0, bsz), im_sl] = ni
            return nr, ni

        hr, hi = lax.fori_loop(0, tt, step, (state_scr[:, re_sl], state_scr[:, im_sl]), unroll=4)
        state_scr[:, re_sl] = hr
        state_scr[:, im_sl] = hi

    for r in range(rows // row_chunk):
        sl = slice(r * row_chunk, (r + 1) * row_chunk)
        y_scr[sl, :] = _glu_tail(h_scr[sl, :].astype(BF16), u_scr[sl, :], cfull, dvec, wglu).astype(BF16)

    ya_bt = _dot(permt_ref[...], y_scr[...]).astype(BF16)
    for b in range(bsz):
        ya_o[b] = ya_bt[b * tt:(b + 1) * tt]

    @pl.when(c == pl.num_programs(0) - 1)
    def _():
        hre_o[...] = state_scr[:, :SSM_COLS]
        him_o[...] = state_scr[:, SSM_COLS:]


def _ssm_prompt(u, ops, wglu):
    bsz, s_len, _ = u.shape
    tt = min(SSM_T_TILE, s_len)
    rows = bsz * tt
    lre = jnp.broadcast_to(ops['lre'], (bsz, SSM_COLS))
    lim = jnp.broadcast_to(ops['lim'], (bsz, SSM_COLS))
    kern = functools.partial(_ssm_prompt_kernel, bsz=bsz, tt=tt)
    src = (jnp.arange(rows, dtype=I32) % bsz) * tt + jnp.arange(rows, dtype=I32) // bsz
    perm = (src[:, None] == jnp.arange(rows, dtype=I32)[None, :]).astype(BF16)
    return pl.pallas_call(
        kern,
        grid=(s_len // tt,),
        in_specs=[pl.BlockSpec((bsz, tt, SSM_WIDTH), lambda c: (0, c, 0)),
                  _const_spec(perm.shape), _const_spec(perm.shape),
                  _const_spec(ops['bfull'].shape), _const_spec(ops['cfull'].shape),
                  _const_spec(lre.shape), _const_spec(lim.shape), _const_spec(ops['d'].shape),
                  _const_spec(wglu.shape)],
        out_specs=[pl.BlockSpec((bsz, tt, SSM_WIDTH), lambda c: (0, c, 0)),
                   _const_spec((bsz, SSM_COLS)), _const_spec((bsz, SSM_COLS))],
        out_shape=[jax.ShapeDtypeStruct((bsz, s_len, SSM_WIDTH), BF16),
                   jax.ShapeDtypeStruct((bsz, SSM_COLS), F32), jax.ShapeDtypeStruct((bsz, SSM_COLS), F32)],
        scratch_shapes=[pltpu.VMEM((rows, 2 * SSM_COLS), F32), pltpu.VMEM((rows, SSM_WIDTH), F32),
                        pltpu.VMEM((rows, SSM_WIDTH), BF16), pltpu.VMEM((bsz, 2 * SSM_COLS), F32)],
        compiler_params=_params(("arbitrary",)),
        name="ssm_prompt",
    )(u, perm, perm.T, ops['bfull'], ops['cfull'], lre, lim, ops['d'], wglu)


def _ssm_sample_kernel(u_ref, h0re_ref, h0im_ref, bfull, cfull, lre_ref, lim_ref, dvec, wglu,
                       ya_o, hre_o, him_o):
    u = u_ref[...]
    bu = _dot(u.astype(BF16), bfull[...])
    lre = lre_ref[...]
    lim = lim_ref[...]
    h0r = h0re_ref[...]
    h0i = h0im_ref[...]
    hr = lre * h0r - lim * h0i + bu[:, :SSM_COLS]
    hi = lre * h0i + lim * h0r + bu[:, SSM_COLS:]
    hre_o[...] = hr
    him_o[...] = hi
    h_bf = jnp.concatenate([hr.astype(BF16), hi.astype(BF16)], axis=1)
    ya_o[...] = _glu_tail(h_bf, u, cfull, dvec, wglu).astype(BF16)


def _ssm_sample(u, h0re, h0im, ops, wglu):
    n = u.shape[0]
    args = [u, h0re, h0im, ops['bfull'], ops['cfull'], ops['lre'], ops['lim'], ops['d'], wglu]
    return pl.pallas_call(
        _ssm_sample_kernel,
        grid=(1,),
        in_specs=[_const_spec(a.shape) for a in args],
        out_specs=[_const_spec((n, SSM_WIDTH)), _const_spec((n, SSM_COLS)), _const_spec((n, SSM_COLS))],
        out_shape=[jax.ShapeDtypeStruct((n, SSM_WIDTH), BF16),
                   jax.ShapeDtypeStruct((n, SSM_COLS), F32), jax.ShapeDtypeStruct((n, SSM_COLS), F32)],
        compiler_params=_params(("arbitrary",)),
        name="ssm_sample",
    )(*args)


def _bisect_kth(count_ge, lo, hi, c_lo, c_hi, k):
    def cond(st):
        it, lo, hi, _, _ = st
        return jnp.logical_and(it < 34, jnp.max((lo + 1 != hi).astype(I32)) > 0)

    def body(st):
        it, lo, hi, c_lo, c_hi = st
        mid = (lo >> 1) + (hi >> 1) + (lo & hi & 1)
        cnt = count_ge(mid)
        active = lo + 1 != hi
        up = jnp.logical_and(active, cnt >= k)
        down = jnp.logical_and(active, cnt < k)
        n_lo = jnp.where(up, mid, lo)
        n_clo = jnp.where(up, cnt, c_lo)
        n_hi = jnp.where(down, mid, jnp.where(jnp.logical_and(active, cnt == k), mid + 1, hi))
        n_chi = jnp.where(down, cnt, c_hi)
        return it + 1, n_lo, n_hi, n_clo, n_chi

    _, lo, hi, c_lo, c_hi = lax.while_loop(cond, body, (jnp.int32(0), lo, hi, c_lo, c_hi))
    return lo, c_lo, c_hi


def _bisect_tie_index(count_tie_le, need, n_tie_rows, idx_hi, n_bits):
    def cond(st):
        it, _, _ = st
        return jnp.logical_and(it < n_bits, n_tie_rows > 0)

    def body(st):
        it, lo, hi = st
        mid = (lo + hi) >> 1
        ok = count_tie_le(mid) >= need
        return it + 1, jnp.where(ok, lo, mid), jnp.where(ok, mid, hi)

    lo0 = jnp.full_like(need, -1)
    hi0 = jnp.full_like(need, idx_hi)
    _, _, hi = lax.while_loop(cond, body, (jnp.int32(0), lo0, hi0))
    return hi


def _attn_prompt_kernel(qt_ref, qit_ref, wit_ref, k_ref, vt_ref, ki_ref, bias_ref, bfar_ref, yb_o,
                        s_scr, o_scr, qm_scr, m_scr, l_scr, a_scr, x_scr, *, k_sel):
    i = pl.program_id(1)
    n_tiles = i + 1
    tq = Q_TILE
    tk = KV_TILE
    q_pos = i * tq + lax.broadcasted_iota(I32, (1, tq), 1)
    kv_iota = lax.broadcasted_iota(I32, (tk, tq), 0)

    def score_tile(j, top):
        r0 = pl.multiple_of(j * tk, tk)
        ki_t = ki_ref[0, pl.ds(r0, tk), :]
        acc = jnp.zeros((tk, tq), F32)
        for h in range(IDX_HEADS):
            s = _dot(ki_t, qit_ref[0, h * IDX_DIM:(h + 1) * IDX_DIM, :])
            acc = acc + wit_ref[0, h:h + 1, :] * jnp.maximum(s, 0.0)
        acc = jnp.where(kv_iota + r0 <= q_pos, acc, -jnp.inf)
        s_scr[pl.ds(r0, tk), :] = _ordered_key(acc)
        return jnp.maximum(top, jnp.max(acc, axis=0, keepdims=True))

    top = lax.fori_loop(0, n_tiles, score_tile, jnp.full((1, tq), -jnp.inf, F32))

    @pl.when(n_tiles % 2 == 1)
    def _():
        s_scr[pl.ds(pl.multiple_of(n_tiles * tk, tk), tk), :] = jnp.full((tk, tq), KEY_NEG_INF, I32)

    def count_tiles(pred):
        def tile(jj, acc):
            for half in range(2):
                r0 = pl.multiple_of((2 * jj + half) * tk, tk)
                ind = pred(s_scr[pl.ds(r0, tk), :], r0).astype(I32)
                for r in range(tk // 8):
                    acc = acc + ind[r * 8:(r + 1) * 8]
            return acc
        acc = lax.fori_loop(0, (n_tiles + 1) // 2, tile, jnp.zeros((8, tq), I32))
        return jnp.sum(acc, axis=0, keepdims=True)

    count_ge = lambda mid: count_tiles(lambda t, r0: t >= mid)
    n_valid = q_pos + 1
    searching = n_valid > k_sel
    floor_key = jnp.full((1, tq), KEY_NEG_INF + 1, I32)
    key_top = _ordered_key(top)
    guess = jnp.maximum(key_top - (4 << 23), 0)
    c_guess = count_ge(guess)
    use_guess = jnp.logical_and(jnp.logical_and(searching, key_top > 0), c_guess >= k_sel)
    lo0 = jnp.where(use_guess, guess, floor_key)
    c_lo0 = jnp.where(use_guess, c_guess, n_valid)
    hi0 = jnp.where(searching, key_top + 1, floor_key + 1)
    thr, c_thr, c_above = _bisect_kth(count_ge, lo0, hi0, c_lo0, jnp.zeros((1, tq), I32), k_sel)
    has_tie = jnp.logical_and(searching, c_thr > k_sel)
    need = k_sel - c_above
    tie_j = _bisect_tie_index(
        lambda mid: count_tiles(lambda t, r0: jnp.logical_and(t == thr, kv_iota + r0 <= mid)),
        need, jnp.max(has_tie.astype(I32)), n_tiles * tk - 1, 14)
    tie_j = jnp.where(has_tie, tie_j, KEY_MAX)

    def mask_tile(j, _):
        r0 = pl.multiple_of(j * tk, tk)
        t = s_scr[pl.ds(r0, tk), :]
        tie_ok = jnp.where(kv_iota + r0 <= tie_j, 0.0, NEG_BIG)
        madd = jnp.where(t > thr, 0.0, jnp.where(t == thr, tie_ok, NEG_BIG))
        s_scr[pl.ds(r0, tk), :] = pltpu.bitcast(madd.astype(F32), I32)
        return 0

    lax.fori_loop(0, n_tiles, mask_tile, 0)

    row = lax.broadcasted_iota(I32, (2 * HEAD_DIM, tq), 0)
    for pair in range(N_HEADS // 2):
        q_pair = qt_ref[0, pair * 128:(pair + 1) * 128, :]
        zero = jnp.zeros_like(q_pair)
        qm_scr[2 * pair] = jnp.where(row < HEAD_DIM, q_pair, zero)
        qm_scr[2 * pair + 1] = jnp.where(row >= HEAD_DIM, q_pair, zero)
    m_scr[...] = jnp.full(m_scr.shape, NEG_BIG, F32)
    l_scr[...] = jnp.zeros(l_scr.shape, F32)
    o_scr[...] = jnp.zeros(o_scr.shape, F32)

    def attn_tile(j, near_kind):
        r0 = pl.multiple_of(j * tk, tk)
        for pair in range(N_HEADS // 2):
            k_t = k_ref[0, pl.ds(r0, tk), pair * 128:(pair + 1) * 128]
            for h in (2 * pair, 2 * pair + 1):
                x = _dot(k_t, qm_scr[h]) + pltpu.bitcast(s_scr[pl.ds(r0, tk), :], F32)
                if near_kind is None:
                    c = bfar_ref[h]
                else:
                    x = x + bias_ref[h, near_kind]
                    c = 0.0
                x_scr[h] = x
                m = m_scr[h]
                m_new = jnp.maximum(m, jnp.max(x, axis=0, keepdims=True) + c)
                a_scr[h] = jnp.exp2(m - m_new)
                m_scr[h] = m_new
        for h in range(N_HEADS):
            alpha = a_scr[h]
            c = bfar_ref[h] if near_kind is None else 0.0
            p = jnp.exp2(x_scr[h] - (m_scr[h] - c))
            l_scr[h] = alpha * l_scr[h] + jnp.sum(p, axis=0, keepdims=True)
            v_t = vt_ref[0, h * HEAD_DIM:(h + 1) * HEAD_DIM, pl.ds(r0, tk)]
            hs = slice(h * HEAD_DIM, (h + 1) * HEAD_DIM)
            o_scr[hs, :] = alpha * o_scr[hs, :] + _dot(v_t, p.astype(BF16))

    def far_tile(j, _):
        attn_tile(j, None)
        return 0

    lax.fori_loop(0, jnp.maximum(i - 1, 0), far_tile, 0)

    @pl.when(i >= 1)
    def _():
        attn_tile(i - 1, 1)

    attn_tile(i, 0)

    for h in range(N_HEADS):
        hs = slice(h * HEAD_DIM, (h + 1) * HEAD_DIM)
        o_scr[hs, :] = o_scr[hs, :] / l_scr[h]
    yb_o[0] = o_scr[...].T.astype(BF16)


def _attn_prompt(qt, qit, wit, k_bf, vt, ki_bf, bias_t, bias_far):
    bsz, _, s_len = qt.shape
    assert s_len % Q_TILE == 0 and Q_TILE == KV_TILE
    qblk = lambda c: pl.BlockSpec((1, c, Q_TILE), lambda b, i: (b, 0, i))
    return pl.pallas_call(
        functools.partial(_attn_prompt_kernel, k_sel=min(TOPK_MAX, s_len // 4)),
        grid=(bsz, s_len // Q_TILE),
        in_specs=[qblk(ATT_WIDTH), qblk(IDX_HEADS * IDX_DIM), qblk(IDX_HEADS),
                  _resident_spec((1, s_len, ATT_WIDTH), lambda b, i: (b, 0, 0)),
                  _resident_spec((1, ATT_WIDTH, s_len), lambda b, i: (b, 0, 0)),
                  _resident_spec((1, s_len, IDX_DIM), lambda b, i: (b, 0, 0)),
                  _resident_spec(bias_t.shape, lambda b, i: (0, 0, 0, 0)),
                  pl.BlockSpec(memory_space=pltpu.SMEM)],
        out_specs=pl.BlockSpec((1, Q_TILE, ATT_WIDTH), lambda b, i: (b, i, 0)),
        out_shape=jax.ShapeDtypeStruct((bsz, s_len, ATT_WIDTH), BF16),
        scratch_shapes=[pltpu.VMEM((pl.cdiv(s_len, 2 * KV_TILE) * 2 * KV_TILE, Q_TILE), I32),
                        pltpu.VMEM((ATT_WIDTH, Q_TILE), F32),
                        pltpu.VMEM((N_HEADS, 2 * HEAD_DIM, Q_TILE), BF16),
                        pltpu.VMEM((N_HEADS, 1, Q_TILE), F32), pltpu.VMEM((N_HEADS, 1, Q_TILE), F32),
                        pltpu.VMEM((N_HEADS, 1, Q_TILE), F32), pltpu.VMEM((N_HEADS, KV_TILE, Q_TILE), F32)],
        compiler_params=_params(("parallel", "arbitrary")),
        name="attn_prompt",
    )(qt, qit, wit, k_bf, vt, ki_bf, bias_t, bias_far)


def _sample_scores_kernel(pt_ref, qi_ref, wi_ref, kin_ref, *rest, gp):
    pages = rest[:gp]
    keys_o, newkey_o = rest[gp:]
    qi = qi_ref[0]
    wi = wi_ref[0]
    rows = []
    for g in range(gp):
        s = _dot(qi, pages[g][0].astype(BF16))
        rows.append(jnp.sum(wi * jnp.maximum(s, 0.0), axis=0, keepdims=True))
    keys_o[0] = _ordered_key(jnp.concatenate(rows, axis=0))

    @pl.when(pl.program_id(1) == 0)
    def _():
        s_new = jnp.sum(qi.astype(F32) * kin_ref[0].astype(F32), axis=1, keepdims=True)
        key_new = _ordered_key(jnp.sum(wi * jnp.maximum(s_new, 0.0), axis=0, keepdims=True))
        newkey_o[0] = jnp.broadcast_to(key_new, (1, PAGE_SIZE))


def _sample_scores(page_table, qi, wi, ki_new, cache_idx_t):
    n, n_pages = page_table.shape
    gp = min(SAMPLE_IDX_PAGES, n_pages)
    assert n_pages % gp == 0
    page_spec = lambda g: pl.BlockSpec((1, IDX_DIM, PAGE_SIZE), lambda b, c, pt: (pt[b, c * gp + g], 0, 0))
    per_seq = lambda shape: pl.BlockSpec((1,) + shape, lambda b, c, pt: (b, 0, 0))
    grid_spec = pltpu.PrefetchScalarGridSpec(
        num_scalar_prefetch=1,
        grid=(n, n_pages // gp),
        in_specs=[per_seq((IDX_HEADS, IDX_DIM)), per_seq((IDX_HEADS, 1)), per_seq((1, IDX_DIM))]
        + [page_spec(g) for g in range(gp)],
        out_specs=[pl.BlockSpec((1, gp, PAGE_SIZE), lambda b, c, pt: (b, c, 0)), per_seq((1, PAGE_SIZE))],
    )
    return pl.pallas_call(
        functools.partial(_sample_scores_kernel, gp=gp),
        grid_spec=grid_spec,
        out_shape=[jax.ShapeDtypeStruct((n, n_pages, PAGE_SIZE), I32), jax.ShapeDtypeStruct((n, 1, PAGE_SIZE), I32)],
        compiler_params=_params(("parallel", "arbitrary")),
        name="sample_scores",
    )(page_table, qi, wi, ki_new, *([cache_idx_t] * gp))


def _sample_select_kernel(keys_ref, newkey_ref, madd_o, newm_o, *, k_sel):
    n, past = keys_ref.shape
    key_new = newkey_ref[:, :1]
    idx = lax.broadcasted_iota(I32, (n, past), 1)

    def total(ind_past, ind_new):
        return jnp.sum(ind_past.astype(I32), axis=1, keepdims=True) + ind_new.astype(I32)

    lo0 = jnp.full((n, 1), KEY_NEG_INF + 1, I32)
    hi0 = jnp.full((n, 1), KEY_MAX, I32)
    thr, c_thr, c_above = _bisect_kth(lambda mid: total(keys_ref[...] >= mid, key_new >= mid),
                                      lo0, hi0, jnp.full((n, 1), past + 1, I32), jnp.zeros((n, 1), I32), k_sel)
    has_tie = c_thr > k_sel
    need = k_sel - c_above
    tie_j = _bisect_tie_index(
        lambda mid: total(jnp.logical_and(keys_ref[...] == thr, idx <= mid),
                          jnp.logical_and(key_new == thr, past <= mid)),
        need, jnp.max(has_tie.astype(I32)), past, 14)
    tie_j = jnp.where(has_tie, tie_j, KEY_MAX)
    t = keys_ref[...]
    tie_ok = jnp.where(idx <= tie_j, 0.0, NEG_BIG)
    madd_o[...] = jnp.where(t > thr, 0.0, jnp.where(t == thr, tie_ok, NEG_BIG)).astype(F32)
    new_ok = jnp.where(past <= tie_j, 0.0, NEG_BIG)
    newm = jnp.where(key_new > thr, 0.0, jnp.where(key_new == thr, new_ok, NEG_BIG)).astype(F32)
    newm_o[...] = jnp.broadcast_to(newm, (n, PAGE_SIZE))


def _sample_select(keys, newkey):
    n, past = keys.shape
    k_sel = min(TOPK_MAX, (past + 1) // 4)
    return pl.pallas_call(
        functools.partial(_sample_select_kernel, k_sel=k_sel),
        grid=(1,),
        in_specs=[_const_spec(keys.shape), _const_spec(newkey.shape)],
        out_specs=[_const_spec(keys.shape), _const_spec(newkey.shape)],
        out_shape=[jax.ShapeDtypeStruct(keys.shape, F32), jax.ShapeDtypeStruct(newkey.shape, F32)],
        compiler_params=_params(("arbitrary",)),
        name="sample_select",
    )(keys, newkey)


def _attn_sample_kernel(pt_ref, q_ref, kn_ref, vn_ref, madd_ref, newm_ref, bias_ref, biasn_ref, *rest, n_pages):
    gp = SAMPLE_KV_PAGES
    k_pages = rest[:gp]
    v_pages = rest[gp:2 * gp]
    yb_o, m_scr, l_scr, acc_scr = rest[2 * gp:]
    c = pl.program_id(1)
    n_chunks = n_pages // gp
    head_of_lane = lax.broadcasted_iota(I32, (N_HEADS, ATT_WIDTH), 1) // HEAD_DIM
    head_of_row = lax.broadcasted_iota(I32, (N_HEADS, ATT_WIDTH), 0)
    own = head_of_lane == head_of_row
    q_blk = jnp.where(own, jnp.broadcast_to(q_ref[0], (N_HEADS, ATT_WIDTH)), 0.0)

    @pl.when(c == 0)
    def _():
        kn = kn_ref[0].astype(BF16).astype(F32)
        x_new = (jnp.sum(q_blk.astype(BF16).astype(F32) * kn, axis=1, keepdims=True)
                 + biasn_ref[...] + newm_ref[0][:, :1])
        m_scr[...] = x_new
        l_scr[...] = jnp.ones_like(l_scr)
        acc_scr[...] = jnp.broadcast_to(vn_ref[0].astype(BF16).astype(F32), (N_HEADS, ATT_WIDTH))

    q_bf = q_blk.astype(BF16)
    xs = []
    for g in range(gp):
        pg = c * gp + g
        kt = k_pages[g][0].astype(BF16)
        xs.append(_dot(q_bf, kt) + madd_ref[0, pg] + bias_ref[pg])
    x = jnp.concatenate(xs, axis=1)
    m = m_scr[...]
    m_new = jnp.maximum(m, jnp.max(x, axis=1, keepdims=True))
    alpha = jnp.exp2(m - m_new)
    p = jnp.exp2(x - m_new)
    l_scr[...] = alpha * l_scr[...] + jnp.sum(p, axis=1, keepdims=True)
    p_bf = p.astype(BF16)
    pv = jnp.zeros((N_HEADS, ATT_WIDTH), F32)
    for g in range(gp):
        vt = v_pages[g][0].astype(BF16)
        pv = pv + _dot_nt(p_bf[:, g * PAGE_SIZE:(g + 1) * PAGE_SIZE], vt)
    acc_scr[...] = alpha * acc_scr[...] + pv
    m_scr[...] = m_new

    @pl.when(c == n_chunks - 1)
    def _():
        o = acc_scr[...] / l_scr[...]
        yb_o[0] = jnp.sum(jnp.where(own, o, 0.0), axis=0, keepdims=True).astype(BF16)


def _attn_sample(page_table, q, k_new, v_new, madd, newm, bias_pages, bias_new, cache_kt, cache_vt):
    n, n_pages = page_table.shape
    gp = SAMPLE_KV_PAGES
    assert n_pages % gp == 0
    page_spec = lambda g: pl.BlockSpec((1, ATT_WIDTH, PAGE_SIZE), lambda b, c, pt: (pt[b, c * gp + g], 0, 0))
    per_seq = lambda shape: pl.BlockSpec((1,) + shape, lambda b, c, pt: (b, 0, 0))
    grid_spec = pltpu.PrefetchScalarGridSpec(
        num_scalar_prefetch=1,
        grid=(n, n_pages // gp),
        in_specs=[per_seq((1, ATT_WIDTH)), per_seq((1, ATT_WIDTH)), per_seq((1, ATT_WIDTH)),
                  pl.BlockSpec((1, n_pages, 1, PAGE_SIZE), lambda b, c, pt: (b, 0, 0, 0)), per_seq((1, PAGE_SIZE)),
                  pl.BlockSpec(bias_pages.shape, lambda b, c, pt: (0, 0, 0)),
                  pl.BlockSpec(bias_new.shape, lambda b, c, pt: (0, 0))]
        + [page_spec(g) for g in range(gp)] * 2,
        out_specs=per_seq((1, ATT_WIDTH)),
        scratch_shapes=[pltpu.VMEM((N_HEADS, 1), F32), pltpu.VMEM((N_HEADS, 1), F32),
                        pltpu.VMEM((N_HEADS, ATT_WIDTH), F32)],
    )
    return pl.pallas_call(
        functools.partial(_attn_sample_kernel, n_pages=n_pages),
        grid_spec=grid_spec,
        out_shape=jax.ShapeDtypeStruct((n, 1, ATT_WIDTH), BF16),
        compiler_params=_params(("parallel", "arbitrary")),
        name="attn_sample",
    )(page_table, q, k_new, v_new, madd, newm, bias_pages, bias_new, *([cache_kt] * gp), *([cache_vt] * gp))


def _merge_kernel(x_ref, ya_ref, yb_ref, sga_ref, sgb_ref, wa, wb, wo, g_ref, b_ref, x1_o, *, alpha):
    merged = (sga_ref[...].astype(F32) * _dot(ya_ref[...], wa[...])
              + sgb_ref[...].astype(F32) * _dot(yb_ref[...], wb[...]))
    z = alpha * x_ref[...] + _dot(merged.astype(BF16), wo[...])
    x1_o[...] = _layer_norm(z, g_ref[...], b_ref[...])


def _merge(x, ya, yb, sga, sgb, wa, wb, wo, g, b, alpha):
    n = x.shape[0]
    tm = min(TOKEN_TILE, n)
    rows = lambda c: pl.BlockSpec((tm, c), lambda i: (i, 0))
    return pl.pallas_call(
        functools.partial(_merge_kernel, alpha=alpha),
        grid=(n // tm,),
        in_specs=[rows(D_MODEL), rows(SSM_WIDTH), rows(ATT_WIDTH), rows(D_MODEL), rows(D_MODEL),
                  _const_spec(wa.shape), _const_spec(wb.shape), _const_spec(wo.shape),
                  _const_spec(g.shape), _const_spec(b.shape)],
        out_specs=rows(D_MODEL),
        out_shape=jax.ShapeDtypeStruct((n, D_MODEL), F32),
        compiler_params=_params(("parallel",)),
        name="merge_out_proj",
    )(x, ya, yb, sga, sgb, wa, wb, wo, g, b)


def _ffn_kernel(x1_ref, wg, wu, wd, g_ref, b_ref, y_o, acc_scr, *, alpha):
    x1 = x1_ref[...]
    xb = x1.astype(BF16)
    acc_scr[...] = alpha * x1
    for c in range(D_FF // FF_CHUNK):
        sl = slice(c * FF_CHUNK, (c + 1) * FF_CHUNK)
        gate = _dot(xb, wg[:, sl])
        up = _dot(xb, wu[:, sl])
        hid = (jax.nn.silu(gate) * up).astype(BF16)
        acc_scr[...] += _dot(hid, wd[sl, :])
    y_o[...] = _layer_norm(acc_scr[...], g_ref[...], b_ref[...])


def _ffn(x1, wg, wu, wd, g, b, alpha):
    n = x1.shape[0]
    tm = min(TOKEN_TILE, n)
    rows = pl.BlockSpec((tm, D_MODEL), lambda i: (i, 0))
    return pl.pallas_call(
        functools.partial(_ffn_kernel, alpha=alpha),
        grid=(n // tm,),
        in_specs=[rows, _const_spec(wg.shape), _const_spec(wu.shape), _const_spec(wd.shape),
                  _const_spec(g.shape), _const_spec(b.shape)],
        out_specs=rows,
        out_shape=jax.ShapeDtypeStruct((n, D_MODEL), F32),
        scratch_shapes=[pltpu.VMEM((tm, D_MODEL), F32)],
        compiler_params=_params(("parallel",)),
        name="ffn",
    )(x1, wg, wu, wd, g, b)


def _rel_bucket(rel):
    n = jnp.maximum(rel, 0)
    max_exact = REL_BUCKETS // 2
    nf = jnp.maximum(n, max_exact).astype(F32)
    large = max_exact + (jnp.log(nf / max_exact) / math.log(REL_MAX_DIST / max_exact)
                         * (REL_BUCKETS - max_exact)).astype(I32)
    large = jnp.minimum(large, REL_BUCKETS - 1)
    return jnp.where(n < max_exact, n, large)


def _bias_lookup(rel_bias, rel):
    onehot = (_rel_bucket(rel)[..., None] == jnp.arange(REL_BUCKETS, dtype=I32)).astype(F32)
    table = rel_bias.astype(F32) * LOG2E
    return jnp.einsum('...k,kh->...h', onehot, table, precision=lax.Precision.HIGHEST)


def _prompt_bias_tiles(rel_bias):
    kl = jnp.arange(KV_TILE, dtype=I32)[:, None]
    ql = jnp.arange(Q_TILE, dtype=I32)[None, :]
    rel = jnp.stack([ql - kl, ql - kl + KV_TILE])
    assert KV_TILE + 1 >= REL_MAX_DIST
    near = jnp.moveaxis(_bias_lookup(rel_bias, rel), -1, 0)
    far = _bias_lookup(rel_bias, jnp.full((), 2 * KV_TILE, I32))
    return near, far


def _ssm_operators(a_re, a_im, log_dt, b_re, b_im, c_re, c_im, d):
    g, p, gi = SSM_GROUPS, SSM_STATE, SSM_GROUP
    a_re, a_im = a_re.astype(F32), a_im.astype(F32)
    dt = jnp.exp(log_dt.astype(F32))[:, None]
    mag = jnp.exp(dt * a_re)
    lre = mag * jnp.cos(dt * a_im)
    lim = mag * jnp.sin(dt * a_im)
    den = a_re * a_re + a_im * a_im
    fr = ((lre - 1.0) * a_re + lim * a_im) / den
    fi = (lim * a_re - (lre - 1.0) * a_im) / den
    bbr = fr[:, :, None] * b_re - fi[:, :, None] * b_im
    bbi = fr[:, :, None] * b_im + fi[:, :, None] * b_re
    eye = jnp.eye(g, dtype=F32)
    blockdiag_in = lambda m: jnp.einsum('gpi,gh->gihp', m, eye).reshape(g * gi, g * p)
    blockdiag_out = lambda m: jnp.einsum('gip,gh->gphi', m, eye).reshape(g * p, g * gi)
    bfull = jnp.concatenate([blockdiag_in(bbr), blockdiag_in(bbi)], axis=1)
    cfull = jnp.concatenate([blockdiag_out(c_re.astype(F32)), blockdiag_out(-c_im.astype(F32))], axis=0)
    return {'lre': lre.reshape(1, g * p), 'lim': lim.reshape(1, g * p),
            'bfull': bfull.astype(BF16), 'cfull': cfull.astype(BF16), 'd': d.astype(F32).reshape(1, g * gi)}


def _split_w_in(w_in):
    offs = [0]
    for n in IN_SPLITS:
        offs.append(offs[-1] + n)
    names = ['u', 'q', 'k', 'v', 'qi', 'ki', 'wi', 'ga', 'gb']
    return {nm: w_in[:, offs[j]:offs[j + 1]] for j, nm in enumerate(names)}


def kernel(x_prompt, x_sample, cache_k, cache_v, cache_idx_k, state_ssm_re, state_ssm_im, page_table, w_in, ssm_a_re, ssm_a_im, ssm_log_dt, ssm_b_re, ssm_b_im, ssm_c_re, ssm_c_im, ssm_d, w_glu, w_branch_a, w_branch_b, rel_bias, w_out, ln1_g, ln1_b, w_ffn_in, w_ffn_out, ln2_g, ln2_b):
    depth = w_in.shape[0]
    bsz, s_len, _ = x_prompt.shape
    n_dec, dec_seq, _ = x_sample.shape
    assert dec_seq == 1
    n_pages = page_table.shape[1]
    n_pool = cache_k.shape[1]
    past = n_pages * PAGE_SIZE
    alpha = (2 * depth) ** 0.25
    idx_scale = IDX_DIM ** -0.5 * IDX_HEADS ** -0.5
    att_scale = HEAD_DIM ** -0.5 * LOG2E

    bias_t, bias_far = _prompt_bias_tiles(rel_bias)
    key_pos = jnp.arange(past, dtype=I32).reshape(n_pages, PAGE_SIZE)
    bias_pages = jnp.swapaxes(_bias_lookup(rel_bias, past - key_pos), 1, 2)
    bias_new = _bias_lookup(rel_bias, jnp.zeros((), I32)).reshape(N_HEADS, 1)

    x_p = x_prompt
    x_s = x_sample.reshape(n_dec, D_MODEL)
    outs = {k: [] for k in ('kp', 'vp', 'kip', 'hpr', 'hpi', 'ks', 'vs', 'kis', 'hsr', 'hsi')}
    for l in range(depth):
        w = _split_w_in(w_in[l])
        wn = {nm: w[nm].astype(BF16) for nm in ('u', 'k', 'v', 'ki', 'ga', 'gb')}
        wn['tq'] = (w['q'] * att_scale).T.astype(BF16)
        wn['tv'] = w['v'].T.astype(BF16)
        wn['tqi'] = w['qi'].T.astype(BF16)
        wn['twi'] = jnp.pad((w['wi'] * idx_scale).T, ((0, 16 - IDX_HEADS), (0, 0))).astype(BF16)
        ws = {nm: w[nm].astype(BF16) for nm in ('u', 'k', 'v', 'qi', 'ki', 'ga', 'gb')}
        ws['q'] = (w['q'] * att_scale).astype(BF16)
        ws['wi'] = (w['wi'] * idx_scale).astype(BF16)
        ops = _ssm_operators(ssm_a_re[l], ssm_a_im[l], ssm_log_dt[l], ssm_b_re[l], ssm_b_im[l],
                             ssm_c_re[l], ssm_c_im[l], ssm_d[l])
        wglu = w_glu[l].astype(BF16)
        wa, wb, wo = w_branch_a[l].astype(BF16), w_branch_b[l].astype(BF16), w_out[l].astype(BF16)
        wg, wu = w_ffn_in[l][:, :D_FF].astype(BF16), w_ffn_in[l][:, D_FF:].astype(BF16)
        wd = w_ffn_out[l].astype(BF16)
        g1, b1 = ln1_g[l].astype(F32).reshape(1, D_MODEL), ln1_b[l].astype(F32).reshape(1, D_MODEL)
        g2, b2 = ln2_g[l].astype(F32).reshape(1, D_MODEL), ln2_b[l].astype(F32).reshape(1, D_MODEL)

        u, k, v, ki, k_bf, ki_bf, sga, sgb, qt, vt, qit, wit = _in_proj_prompt(x_p, wn)
        ya, hpr, hpi = _ssm_prompt(u, ops, wglu)
        yb = _attn_prompt(qt, qit, wit, k_bf, vt, ki_bf, bias_t, bias_far)
        n_tok = bsz * s_len
        flat = lambda a: a.reshape(n_tok, a.shape[-1])
        x1 = _merge(flat(x_p), flat(ya), flat(yb), flat(sga), flat(sgb), wa, wb, wo, g1, b1, alpha)
        x_p = _ffn(x1, wg, wu, wd, g2, b2, alpha).reshape(bsz, s_len, D_MODEL)
        outs['kp'].append(k.reshape(bsz, s_len, N_HEADS, HEAD_DIM))
        outs['vp'].append(v.reshape(bsz, s_len, N_HEADS, HEAD_DIM))
        outs['kip'].append(ki)
        outs['hpr'].append(hpr.reshape(bsz, SSM_GROUPS, SSM_STATE))
        outs['hpi'].append(hpi.reshape(bsz, SSM_GROUPS, SSM_STATE))

        us, qs, k_s, v_s, qis, kis, wis, sga_s, sgb_s = _in_proj_sample(x_s, ws)
        ya_s, hsr, hsi = _ssm_sample(us, state_ssm_re[l].reshape(n_dec, SSM_COLS).astype(F32),
                                     state_ssm_im[l].reshape(n_dec, SSM_COLS).astype(F32), ops, wglu)
        cache_kt = jnp.transpose(cache_k[l], (0, 2, 3, 1)).reshape(n_pool, ATT_WIDTH, PAGE_SIZE)
        cache_vt = jnp.transpose(cache_v[l], (0, 2, 3, 1)).reshape(n_pool, ATT_WIDTH, PAGE_SIZE)
        cache_it = jnp.transpose(cache_idx_k[l], (0, 2, 1))
        keys, newkey = _sample_scores(
            page_table, qis.astype(BF16).reshape(n_dec, IDX_HEADS, IDX_DIM), wis.reshape(n_dec, IDX_HEADS, 1),
            kis.astype(BF16).reshape(n_dec, 1, IDX_DIM), cache_it)
        madd, newm = _sample_select(keys.reshape(n_dec, past), newkey.reshape(n_dec, PAGE_SIZE))
        yb_s = _attn_sample(page_table, qs.reshape(n_dec, 1, ATT_WIDTH), k_s.reshape(n_dec, 1, ATT_WIDTH),
                            v_s.reshape(n_dec, 1, ATT_WIDTH), madd.reshape(n_dec, n_pages, 1, PAGE_SIZE),
                            newm.reshape(n_dec, 1, PAGE_SIZE), bias_pages, bias_new, cache_kt, cache_vt)
        x1_s = _merge(x_s, ya_s, yb_s.reshape(n_dec, ATT_WIDTH), sga_s, sgb_s, wa, wb, wo, g1, b1, alpha)
        x_s = _ffn(x1_s, wg, wu, wd, g2, b2, alpha)
        outs['ks'].append(k_s.reshape(n_dec, 1, N_HEADS, HEAD_DIM))
        outs['vs'].append(v_s.reshape(n_dec, 1, N_HEADS, HEAD_DIM))
        outs['kis'].append(kis.reshape(n_dec, 1, IDX_DIM))
        outs['hsr'].append(hsr.reshape(n_dec, SSM_GROUPS, SSM_STATE))
        outs['hsi'].append(hsi.reshape(n_dec, SSM_GROUPS, SSM_STATE))

    st = lambda name: jnp.stack(outs[name])
    return (x_p, x_s.reshape(n_dec, 1, D_MODEL),
            st('kp'), st('vp'), st('kip'), st('hpr'), st('hpi'),
            st('ks'), st('vs'), st('kis'), st('hsr'), st('hsi'))
```

```python
import functools
import math

import jax
import jax.numpy as jnp
from jax import lax
from jax.experimental import pallas as pl
from jax.experimental.pallas import tpu as pltpu

F32 = jnp.float32
BF16 = jnp.bfloat16
I32 = jnp.int32

D_MODEL = 1024
SSM_WIDTH = 512
SSM_GROUP = 16
SSM_GROUPS = 32
SSM_STATE = 64
SSM_COLS = SSM_GROUPS * SSM_STATE
N_HEADS = 8
HEAD_DIM = 64
ATT_WIDTH = 512
IDX_HEADS = 8
IDX_DIM = 32
TOPK_MAX = 256
PAGE_SIZE = 128
REL_BUCKETS = 32
REL_MAX_DIST = 128
D_FF = 2816
LN_EPS = 1e-5
IN_SPLITS = (SSM_WIDTH, ATT_WIDTH, ATT_WIDTH, ATT_WIDTH, IDX_HEADS * IDX_DIM, IDX_DIM, IDX_HEADS, D_MODEL, D_MODEL)
LOG2E = math.log2(math.e)

VMEM_LIMIT_BYTES = 56 * 1024 * 1024
NEG_BIG = -1e30
KEY_NEG_INF = -2139095041
KEY_MAX = 2147483647

TOKEN_TILE = 512
Q_TILE = 256
KV_TILE = 256
SSM_T_TILE = 64
FF_CHUNK = 256
SAMPLE_IDX_PAGES = 32
SAMPLE_KV_PAGES = 8


def _params(sem):
    return pltpu.CompilerParams(dimension_semantics=sem, vmem_limit_bytes=VMEM_LIMIT_BYTES)


def _const_spec(shape):
    n = len(shape)
    return pl.BlockSpec(shape, lambda *_: (0,) * n)


def _resident_spec(block, index_map):
    return pl.BlockSpec(block, index_map, pipeline_mode=pl.Buffered(1))


def _dot(a, b):
    return jnp.dot(a, b, preferred_element_type=F32)


def _dot_nt(a, b):
    return lax.dot_general(a, b, (((1,), (1,)), ((), ())), preferred_element_type=F32)


def _layer_norm(z, g, b):
    mu = jnp.mean(z, axis=-1, keepdims=True)
    zc = z - mu
    var = jnp.mean(zc * zc, axis=-1, keepdims=True)
    return zc * lax.rsqrt(var + LN_EPS) * g + b


def _ordered_key(x):
    b = pltpu.bitcast(x, I32)
    return b ^ ((b >> 31) & KEY_MAX)


def _in_proj_prompt_kernel(x_ref, wu, wk, wv, wga, wgb, wki, wtq, wtv, wtqi, wtwi,
                           u_o, k_o, v_o, ki_o, kbf_o, kibf_o, sga_o, sgb_o, qt_o, vt_o, qit_o, wit_o):
    xb = x_ref[0].astype(BF16)
    u_o[0] = _dot(xb, wu[...])
    k = _dot(xb, wk[...])
    k_o[0] = k
    kbf_o[0] = k.astype(BF16)
    v_o[0] = _dot(xb, wv[...])
    ki = _dot(xb, wki[...])
    ki_o[0] = ki
    kibf_o[0] = ki.astype(BF16)
    sga_o[0] = jax.nn.sigmoid(_dot(xb, wga[...])).astype(BF16)
    sgb_o[0] = jax.nn.sigmoid(_dot(xb, wgb[...])).astype(BF16)
    qt_o[0] = _dot_nt(wtq[...], xb).astype(BF16)
    vt_o[0] = _dot_nt(wtv[...], xb).astype(BF16)
    qit_o[0] = _dot_nt(wtqi[...], xb).astype(BF16)
    wit_o[0] = _dot_nt(wtwi[...], xb)[:IDX_HEADS]


def _in_proj_prompt(x, w):
    bsz, s_len, _ = x.shape
    tm = min(TOKEN_TILE, s_len)
    nat = lambda c: pl.BlockSpec((1, tm, c), lambda b, i: (b, i, 0))
    tr = lambda c: pl.BlockSpec((1, c, tm), lambda b, i: (b, 0, i))
    sds = jax.ShapeDtypeStruct
    weights = [w['u'], w['k'], w['v'], w['ga'], w['gb'], w['ki'], w['tq'], w['tv'], w['tqi'], w['twi']]
    return pl.pallas_call(
        _in_proj_prompt_kernel,
        grid=(bsz, s_len // tm),
        in_specs=[nat(D_MODEL)] + [_const_spec(a.shape) for a in weights],
        out_specs=[nat(SSM_WIDTH), nat(ATT_WIDTH), nat(ATT_WIDTH), nat(IDX_DIM), nat(ATT_WIDTH), nat(IDX_DIM),
                   nat(D_MODEL), nat(D_MODEL), tr(ATT_WIDTH), tr(ATT_WIDTH), tr(IDX_HEADS * IDX_DIM), tr(IDX_HEADS)],
        out_shape=[sds((bsz, s_len, SSM_WIDTH), F32), sds((bsz, s_len, ATT_WIDTH), F32),
                   sds((bsz, s_len, ATT_WIDTH), F32), sds((bsz, s_len, IDX_DIM), F32),
                   sds((bsz, s_len, ATT_WIDTH), BF16), sds((bsz, s_len, IDX_DIM), BF16),
                   sds((bsz, s_len, D_MODEL), BF16), sds((bsz, s_len, D_MODEL), BF16),
                   sds((bsz, ATT_WIDTH, s_len), BF16), sds((bsz, ATT_WIDTH, s_len), BF16),
                   sds((bsz, IDX_HEADS * IDX_DIM, s_len), BF16), sds((bsz, IDX_HEADS, s_len), F32)],
        compiler_params=_params(("parallel", "parallel")),
        name="in_proj_prompt",
    )(x, *weights)


def _in_proj_sample_kernel(x_ref, wu, wq, wk, wv, wqi, wki, wwi, wga, wgb,
                           u_o, q_o, k_o, v_o, qi_o, ki_o, wi_o, sga_o, sgb_o):
    xb = x_ref[...].astype(BF16)
    u_o[...] = _dot(xb, wu[...])
    q_o[...] = _dot(xb, wq[...])
    k_o[...] = _dot(xb, wk[...])
    v_o[...] = _dot(xb, wv[...])
    qi_o[...] = _dot(xb, wqi[...])
    ki_o[...] = _dot(xb, wki[...])
    wi_o[...] = _dot(xb, wwi[...])
    sga_o[...] = jax.nn.sigmoid(_dot(xb, wga[...])).astype(BF16)
    sgb_o[...] = jax.nn.sigmoid(_dot(xb, wgb[...])).astype(BF16)


def _in_proj_sample(x, w):
    n = x.shape[0]
    weights = [w['u'], w['q'], w['k'], w['v'], w['qi'], w['ki'], w['wi'], w['ga'], w['gb']]
    cols = [SSM_WIDTH, ATT_WIDTH, ATT_WIDTH, ATT_WIDTH, IDX_HEADS * IDX_DIM, IDX_DIM, IDX_HEADS, D_MODEL, D_MODEL]
    dts = [F32] * 7 + [BF16, BF16]
    return pl.pallas_call(
        _in_proj_sample_kernel,
        grid=(1,),
        in_specs=[_const_spec(x.shape)] + [_const_spec(a.shape) for a in weights],
        out_specs=[_const_spec((n, c)) for c in cols],
        out_shape=[jax.ShapeDtypeStruct((n, c), d) for c, d in zip(cols, dts)],
        compiler_params=_params(("arbitrary",)),
        name="in_proj_sample",
    )(x, *weights)


def _glu_tail(h_bf, u, cfull, dvec, wglu):
    y = _dot(h_bf, cfull[...]) + dvec[...] * u
    y = jax.nn.gelu(y)
    return y * jax.nn.sigmoid(_dot(y.astype(BF16), wglu[...]))


def _ssm_prompt_kernel(u_ref, perm_ref, permt_ref, bfull, cfull, lre_ref, lim_ref, dvec, wglu,
                       ya_o, hre_o, him_o, h_scr, u_scr, y_scr, state_scr, *, bsz, tt):
    c = pl.program_id(0)
    rows = bsz * tt
    row_chunk = 256

    @pl.when(c == 0)
    def _():
        state_scr[...] = jnp.zeros_like(state_scr)

    u_bt = jnp.concatenate([u_ref[b] for b in range(bsz)], axis=0)
    u_hi = u_bt.astype(BF16)
    u_lo = (u_bt - u_hi.astype(F32)).astype(BF16)
    u_scr[...] = _dot(perm_ref[...], u_hi) + _dot(perm_ref[...], u_lo)
    for r in range(rows // row_chunk):
        sl = slice(r * row_chunk, (r + 1) * row_chunk)
        h_scr[sl, :] = _dot(u_scr[sl, :].astype(BF16), bfull[...])

    chunk = 512
    for ci in range(SSM_COLS // chunk):
        re_sl = slice(ci * chunk, (ci + 1) * chunk)
        im_sl = slice(SSM_COLS + ci * chunk, SSM_COLS + (ci + 1) * chunk)
        lre = lre_ref[:, re_sl]
        lim = lim_ref[:, re_sl]

        def step(t, carry):
            hr, hi = carry
            r0 = pl.multiple_of(t * bsz, bsz)
            nr = lre * hr - lim * hi + h_scr[pl.ds(r0, bsz), re_sl]
            ni = lre * hi + lim * hr + h_scr[pl.ds(r0, bsz), im_sl]
            h_scr[pl.ds(r0, bsz), re_sl] = nr
            h_scr[pl.ds(r0, bsz), im_sl] = ni
            return nr, ni

        hr, hi = lax.fori_loop(0, tt, step, (state_scr[:, re_sl], state_scr[:, im_sl]), unroll=4)
        state_scr[:, re_sl] = hr
        state_scr[:, im_sl] = hi

    for r in range(rows // row_chunk):
        sl = slice(r * row_chunk, (r + 1) * row_chunk)
        y_scr[sl, :] = _glu_tail(h_scr[sl, :].astype(BF16), u_scr[sl, :], cfull, dvec, wglu).astype(BF16)

    ya_bt = _dot(permt_ref[...], y_scr[...]).astype(BF16)
    for b in range(bsz):
        ya_o[b] = ya_bt[b * tt:(b + 1) * tt]

    @pl.when(c == pl.num_programs(0) - 1)
    def _():
        hre_o[...] = state_scr[:, :SSM_COLS]
        him_o[...] = state_scr[:, SSM_COLS:]


def _ssm_prompt(u, ops, wglu):
    bsz, s_len, _ = u.shape
    tt = min(SSM_T_TILE, s_len)
    rows = bsz * tt
    lre = jnp.broadcast_to(ops['lre'], (bsz, SSM_COLS))
    lim = jnp.broadcast_to(ops['lim'], (bsz, SSM_COLS))
    kern = functools.partial(_ssm_prompt_kernel, bsz=bsz, tt=tt)
    src = (jnp.arange(rows, dtype=I32) % bsz) * tt + jnp.arange(rows, dtype=I32) // bsz
    perm = (src[:, None] == jnp.arange(rows, dtype=I32)[None, :]).astype(BF16)
    return pl.pallas_call(
        kern,
        grid=(s_len // tt,),
        in_specs=[pl.BlockSpec((bsz, tt, SSM_WIDTH), lambda c: (0, c, 0)),
                  _const_spec(perm.shape), _const_spec(perm.shape),
                  _const_spec(ops['bfull'].shape), _const_spec(ops['cfull'].shape),
                  _const_spec(lre.shape), _const_spec(lim.shape), _const_spec(ops['d'].shape),
                  _const_spec(wglu.shape)],
        out_specs=[pl.BlockSpec((bsz, tt, SSM_WIDTH), lambda c: (0, c, 0)),
                   _const_spec((bsz, SSM_COLS)), _const_spec((bsz, SSM_COLS))],
        out_shape=[jax.ShapeDtypeStruct((bsz, s_len, SSM_WIDTH), BF16),
                   jax.ShapeDtypeStruct((bsz, SSM_COLS), F32), jax.ShapeDtypeStruct((bsz, SSM_COLS), F32)],
        scratch_shapes=[pltpu.VMEM((rows, 2 * SSM_COLS), F32), pltpu.VMEM((rows, SSM_WIDTH), F32),
                        pltpu.VMEM((rows, SSM_WIDTH), BF16), pltpu.VMEM((bsz, 2 * SSM_COLS), F32)],
        compiler_params=_params(("arbitrary",)),
        name="ssm_prompt",
    )(u, perm, perm.T, ops['bfull'], ops['cfull'], lre, lim, ops['d'], wglu)


def _ssm_sample_kernel(u_ref, h0re_ref, h0im_ref, bfull, cfull, lre_ref, lim_ref, dvec, wglu,
                       ya_o, hre_o, him_o):
    u = u_ref[...]
    bu = _dot(u.astype(BF16), bfull[...])
    lre = lre_ref[...]
    lim = lim_ref[...]
    h0r = h0re_ref[...]
    h0i = h0im_ref[...]
    hr = lre * h0r - lim * h0i + bu[:, :SSM_COLS]
    hi = lre * h0i + lim * h0r + bu[:, SSM_COLS:]
    hre_o[...] = hr
    him_o[...] = hi
    h_bf = jnp.concatenate([hr.astype(BF16), hi.astype(BF16)], axis=1)
    ya_o[...] = _glu_tail(h_bf, u, cfull, dvec, wglu).astype(BF16)


def _ssm_sample(u, h0re, h0im, ops, wglu):
    n = u.shape[0]
    args = [u, h0re, h0im, ops['bfull'], ops['cfull'], ops['lre'], ops['lim'], ops['d'], wglu]
    return pl.pallas_call(
        _ssm_sample_kernel,
        grid=(1,),
        in_specs=[_const_spec(a.shape) for a in args],
        out_specs=[_const_spec((n, SSM_WIDTH)), _const_spec((n, SSM_COLS)), _const_spec((n, SSM_COLS))],
        out_shape=[jax.ShapeDtypeStruct((n, SSM_WIDTH), BF16),
                   jax.ShapeDtypeStruct((n, SSM_COLS), F32), jax.ShapeDtypeStruct((n, SSM_COLS), F32)],
        compiler_params=_params(("arbitrary",)),
        name="ssm_sample",
    )(*args)


def _bisect_kth(count_ge, lo, hi, c_lo, c_hi, k):
    def cond(st):
        it, lo, hi, _, _ = st
        return jnp.logical_and(it < 34, jnp.max((lo + 1 != hi).astype(I32)) > 0)

    def body(st):
        it, lo, hi, c_lo, c_hi = st
        mid = (lo >> 1) + (hi >> 1) + (lo & hi & 1)
        cnt = count_ge(mid)
        active = lo + 1 != hi
        up = jnp.logical_and(active, cnt >= k)
        down = jnp.logical_and(active, cnt < k)
        n_lo = jnp.where(up, mid, lo)
        n_clo = jnp.where(up, cnt, c_lo)
        n_hi = jnp.where(down, mid, jnp.where(jnp.logical_and(active, cnt == k), mid + 1, hi))
        n_chi = jnp.where(down, cnt, c_hi)
        return it + 1, n_lo, n_hi, n_clo, n_chi

    _, lo, hi, c_lo, c_hi = lax.while_loop(cond, body, (jnp.int32(0), lo, hi, c_lo, c_hi))
    return lo, c_lo, c_hi


def _bisect_tie_index(count_tie_le, need, n_tie_rows, idx_hi, n_bits):
    def cond(st):
        it, _, _ = st
        return jnp.logical_and(it < n_bits, n_tie_rows > 0)

    def body(st):
        it, lo, hi = st
        mid = (lo + hi) >> 1
        ok = count_tie_le(mid) >= need
        return it + 1, jnp.where(ok, lo, mid), jnp.where(ok, mid, hi)

    lo0 = jnp.full_like(need, -1)
    hi0 = jnp.full_like(need, idx_hi)
    _, _, hi = lax.while_loop(cond, body, (jnp.int32(0), lo0, hi0))
    return hi


def _attn_prompt_kernel(qt_ref, qit_ref, wit_ref, k_ref, vt_ref, ki_ref, bias_ref, bfar_ref, yb_o,
                        s_scr, o_scr, qm_scr, m_scr, l_scr, a_scr, off_scr, x_scr, *, k_sel):
    i = pl.program_id(1)
    n_tiles = i + 1
    tq = Q_TILE
    tk = KV_TILE
    q_pos = i * tq + lax.broadcasted_iota(I32, (1, tq), 1)
    kv_iota = lax.broadcasted_iota(I32, (tk, tq), 0)

    n_pairs = (n_tiles + 1) // 2

    def score_pair(jj, top):
        for half in range(2):
            r0 = pl.multiple_of((2 * jj + half) * tk, tk)
            ki_t = ki_ref[0, pl.ds(r0, tk), :]
            acc = jnp.zeros((tk, tq), F32)
            for h in range(IDX_HEADS):
                s = _dot(ki_t, qit_ref[0, h * IDX_DIM:(h + 1) * IDX_DIM, :])
                acc = acc + wit_ref[0, h:h + 1, :] * jnp.maximum(s, 0.0)
            acc = jnp.where(kv_iota + r0 <= q_pos, acc, -jnp.inf)
            s_scr[pl.ds(r0, tk), :] = _ordered_key(acc)
            top = jnp.maximum(top, jnp.max(acc, axis=0, keepdims=True))
        return top

    top = lax.fori_loop(0, n_pairs, score_pair, jnp.full((1, tq), -jnp.inf, F32))

    def count_tiles(pred):
        def tile(jj, acc):
            for half in range(2):
                r0 = pl.multiple_of((2 * jj + half) * tk, tk)
                ind = pred(s_scr[pl.ds(r0, tk), :], r0).astype(I32)
                for r in range(tk // 8):
                    acc = acc + ind[r * 8:(r + 1) * 8]
            return acc
        acc = lax.fori_loop(0, n_pairs, tile, jnp.zeros((8, tq), I32))
        return jnp.sum(acc, axis=0, keepdims=True)

    count_ge = lambda mid: count_tiles(lambda t, r0: t >= mid)
    n_valid = q_pos + 1
    searching = n_valid > k_sel
    floor_key = jnp.full((1, tq), KEY_NEG_INF + 1, I32)
    key_top = _ordered_key(top)
    guess = jnp.maximum(key_top - (4 << 23), 0)
    c_guess = count_ge(guess)
    use_guess = jnp.logical_and(jnp.logical_and(searching, key_top > 0), c_guess >= k_sel)
    lo0 = jnp.where(use_guess, guess, floor_key)
    c_lo0 = jnp.where(use_guess, c_guess, n_valid)
    hi0 = jnp.where(searching, key_top + 1, floor_key + 1)
    thr, c_thr, c_above = _bisect_kth(count_ge, lo0, hi0, c_lo0, jnp.zeros((1, tq), I32), k_sel)
    has_tie = jnp.logical_and(searching, c_thr > k_sel)
    need = k_sel - c_above
    tie_j = _bisect_tie_index(
        lambda mid: count_tiles(lambda t, r0: jnp.logical_and(t == thr, kv_iota + r0 <= mid)),
        need, jnp.max(has_tie.astype(I32)), n_tiles * tk - 1, 14)
    tie_j = jnp.where(has_tie, tie_j, KEY_MAX)

    def mask_pair(jj, _):
        for half in range(2):
            r0 = pl.multiple_of((2 * jj + half) * tk, tk)
            t = s_scr[pl.ds(r0, tk), :]
            tie_ok = jnp.where(kv_iota + r0 <= tie_j, 0.0, NEG_BIG)
            madd = jnp.where(t > thr, 0.0, jnp.where(t == thr, tie_ok, NEG_BIG))
            s_scr[pl.ds(r0, tk), :] = pltpu.bitcast(madd.astype(F32), I32)
        return 0

    lax.fori_loop(0, n_pairs, mask_pair, 0)

    row = lax.broadcasted_iota(I32, (2 * HEAD_DIM, tq), 0)
    for pair in range(N_HEADS // 2):
        q_pair = qt_ref[0, pair * 128:(pair + 1) * 128, :]
        zero = jnp.zeros_like(q_pair)
        qm_scr[2 * pair] = jnp.where(row < HEAD_DIM, q_pair, zero)
        qm_scr[2 * pair + 1] = jnp.where(row >= HEAD_DIM, q_pair, zero)
    m_scr[...] = jnp.full(m_scr.shape, NEG_BIG, F32)
    l_scr[...] = jnp.zeros(l_scr.shape, F32)
    o_scr[...] = jnp.zeros(o_scr.shape, F32)

    def attn_tiles(tiles):
        for slot, (j, near_kind) in enumerate(tiles):
            r0 = pl.multiple_of(j * tk, tk)
            for pair in range(N_HEADS // 2):
                k_t = k_ref[0, pl.ds(r0, tk), pair * 128:(pair + 1) * 128]
                for h in (2 * pair, 2 * pair + 1):
                    x = _dot(k_t, qm_scr[h]) + pltpu.bitcast(s_scr[pl.ds(r0, tk), :], F32)
                    if near_kind is None:
                        c = bfar_ref[h]
                    else:
                        x = x + bias_ref[h, near_kind]
                        c = 0.0
                    x_scr[slot, h] = x
                    m = m_scr[h]
                    m_new = jnp.maximum(m, jnp.max(x, axis=0, keepdims=True) + c)
                    a_scr[slot, h] = jnp.exp2(m - m_new)
                    off_scr[slot, h] = m_new - c
                    m_scr[h] = m_new
        for slot, (j, _) in enumerate(tiles):
            r0 = pl.multiple_of(j * tk, tk)
            for h in range(N_HEADS):
                alpha = a_scr[slot, h]
                p = jnp.exp2(x_scr[slot, h] - off_scr[slot, h])
                l_scr[h] = alpha * l_scr[h] + jnp.sum(p, axis=0, keepdims=True)
                v_t = vt_ref[0, h * HEAD_DIM:(h + 1) * HEAD_DIM, pl.ds(r0, tk)]
                hs = slice(h * HEAD_DIM, (h + 1) * HEAD_DIM)
                o_scr[hs, :] = alpha * o_scr[hs, :] + _dot(v_t, p.astype(BF16))

    n_far = jnp.maximum(i - 1, 0)

    def far_pair(jj, _):
        attn_tiles([(2 * jj, None), (2 * jj + 1, None)])
        return 0

    lax.fori_loop(0, n_far // 2, far_pair, 0)

    @pl.when(n_far % 2 == 1)
    def _():
        attn_tiles([(n_far - 1, None)])

    @pl.when(i >= 1)
    def _():
        attn_tiles([(i - 1, 1), (i, 0)])

    @pl.when(i == 0)
    def _():
        attn_tiles([(i, 0)])

    for h in range(N_HEADS):
        hs = slice(h * HEAD_DIM, (h + 1) * HEAD_DIM)
        o_scr[hs, :] = o_scr[hs, :] / l_scr[h]
    yb_o[0] = o_scr[...].T.astype(BF16)


def _attn_prompt(qt, qit, wit, k_bf, vt, ki_bf, bias_t, bias_far):
    bsz, _, s_len = qt.shape
    assert s_len % (2 * KV_TILE) == 0 and Q_TILE == KV_TILE
    qblk = lambda c: pl.BlockSpec((1, c, Q_TILE), lambda b, i: (b, 0, i))
    return pl.pallas_call(
        functools.partial(_attn_prompt_kernel, k_sel=min(TOPK_MAX, s_len // 4)),
        grid=(bsz, s_len // Q_TILE),
        in_specs=[qblk(ATT_WIDTH), qblk(IDX_HEADS * IDX_DIM), qblk(IDX_HEADS),
                  _resident_spec((1, s_len, ATT_WIDTH), lambda b, i: (b, 0, 0)),
                  _resident_spec((1, ATT_WIDTH, s_len), lambda b, i: (b, 0, 0)),
                  _resident_spec((1, s_len, IDX_DIM), lambda b, i: (b, 0, 0)),
                  _resident_spec(bias_t.shape, lambda b, i: (0, 0, 0, 0)),
                  pl.BlockSpec(memory_space=pltpu.SMEM)],
        out_specs=pl.BlockSpec((1, Q_TILE, ATT_WIDTH), lambda b, i: (b, i, 0)),
        out_shape=jax.ShapeDtypeStruct((bsz, s_len, ATT_WIDTH), BF16),
        scratch_shapes=[pltpu.VMEM((pl.cdiv(s_len, 2 * KV_TILE) * 2 * KV_TILE, Q_TILE), I32),
                        pltpu.VMEM((ATT_WIDTH, Q_TILE), F32),
                        pltpu.VMEM((N_HEADS, 2 * HEAD_DIM, Q_TILE), BF16),
                        pltpu.VMEM((N_HEADS, 1, Q_TILE), F32), pltpu.VMEM((N_HEADS, 1, Q_TILE), F32),
                        pltpu.VMEM((2, N_HEADS, 1, Q_TILE), F32), pltpu.VMEM((2, N_HEADS, 1, Q_TILE), F32),
                        pltpu.VMEM((2, N_HEADS, KV_TILE, Q_TILE), F32)],
        compiler_params=_params(("parallel", "arbitrary")),
        name="attn_prompt",
    )(qt, qit, wit, k_bf, vt, ki_bf, bias_t, bias_far)


def _sample_scores_kernel(pt_ref, qi_ref, wi_ref, kin_ref, *rest, gp):
    pages = rest[:gp]
    keys_o, newkey_o = rest[gp:]
    qi = qi_ref[0]
    wi = wi_ref[0]
    rows = []
    for g in range(gp):
        s = _dot(qi, pages[g][0].astype(BF16))
        rows.append(jnp.sum(wi * jnp.maximum(s, 0.0), axis=0, keepdims=True))
    keys_o[0] = _ordered_key(jnp.concatenate(rows, axis=0))

    @pl.when(pl.program_id(1) == 0)
    def _():
        s_new = jnp.sum(qi.astype(F32) * kin_ref[0].astype(F32), axis=1, keepdims=True)
        key_new = _ordered_key(jnp.sum(wi * jnp.maximum(s_new, 0.0), axis=0, keepdims=True))
        newkey_o[0] = jnp.broadcast_to(key_new, (1, PAGE_SIZE))


def _sample_scores(page_table, qi, wi, ki_new, cache_idx_t):
    n, n_pages = page_table.shape
    gp = min(SAMPLE_IDX_PAGES, n_pages)
    assert n_pages % gp == 0
    page_spec = lambda g: pl.BlockSpec((1, IDX_DIM, PAGE_SIZE), lambda b, c, pt: (pt[b, c * gp + g], 0, 0))
    per_seq = lambda shape: pl.BlockSpec((1,) + shape, lambda b, c, pt: (b, 0, 0))
    grid_spec = pltpu.PrefetchScalarGridSpec(
        num_scalar_prefetch=1,
        grid=(n, n_pages // gp),
        in_specs=[per_seq((IDX_HEADS, IDX_DIM)), per_seq((IDX_HEADS, 1)), per_seq((1, IDX_DIM))]
        + [page_spec(g) for g in range(gp)],
        out_specs=[pl.BlockSpec((1, gp, PAGE_SIZE), lambda b, c, pt: (b, c, 0)), per_seq((1, PAGE_SIZE))],
    )
    return pl.pallas_call(
        functools.partial(_sample_scores_kernel, gp=gp),
        grid_spec=grid_spec,
        out_shape=[jax.ShapeDtypeStruct((n, n_pages, PAGE_SIZE), I32), jax.ShapeDtypeStruct((n, 1, PAGE_SIZE), I32)],
        compiler_params=_params(("parallel", "arbitrary")),
        name="sample_scores",
    )(page_table, qi, wi, ki_new, *([cache_idx_t] * gp))


def _sample_select_kernel(keys_ref, newkey_ref, madd_o, newm_o, *, k_sel):
    n, past = keys_ref.shape
    key_new = newkey_ref[:, :1]
    idx = lax.broadcasted_iota(I32, (n, past), 1)

    def total(ind_past, ind_new):
        return jnp.sum(ind_past.astype(I32), axis=1, keepdims=True) + ind_new.astype(I32)

    lo0 = jnp.full((n, 1), KEY_NEG_INF + 1, I32)
    hi0 = jnp.full((n, 1), KEY_MAX, I32)
    thr, c_thr, c_above = _bisect_kth(lambda mid: total(keys_ref[...] >= mid, key_new >= mid),
                                      lo0, hi0, jnp.full((n, 1), past + 1, I32), jnp.zeros((n, 1), I32), k_sel)
    has_tie = c_thr > k_sel
    need = k_sel - c_above
    tie_j = _bisect_tie_index(
        lambda mid: total(jnp.logical_and(keys_ref[...] == thr, idx <= mid),
                          jnp.logical_and(key_new == thr, past <= mid)),
        need, jnp.max(has_tie.astype(I32)), past, 14)
    tie_j = jnp.where(has_tie, tie_j, KEY_MAX)
    t = keys_ref[...]
    tie_ok = jnp.where(idx <= tie_j, 0.0, NEG_BIG)
    madd_o[...] = jnp.where(t > thr, 0.0, jnp.where(t == thr, tie_ok, NEG_BIG)).astype(F32)
    new_ok = jnp.where(past <= tie_j, 0.0, NEG_BIG)
    newm = jnp.where(key_new > thr, 0.0, jnp.where(key_new == thr, new_ok, NEG_BIG)).astype(F32)
    newm_o[...] = jnp.broadcast_to(newm, (n, PAGE_SIZE))


def _sample_select(keys, newkey):
    n, past = keys.shape
    k_sel = min(TOPK_MAX, (past + 1) // 4)
    return pl.pallas_call(
        functools.partial(_sample_select_kernel, k_sel=k_sel),
        grid=(1,),
        in_specs=[_const_spec(keys.shape), _const_spec(newkey.shape)],
        out_specs=[_const_spec(keys.shape), _const_spec(newkey.shape)],
        out_shape=[jax.ShapeDtypeStruct(keys.shape, F32), jax.ShapeDtypeStruct(newkey.shape, F32)],
        compiler_params=_params(("arbitrary",)),
        name="sample_select",
    )(keys, newkey)


def _attn_sample_kernel(pt_ref, q_ref, kn_ref, vn_ref, madd_ref, newm_ref, bias_ref, biasn_ref, *rest, n_pages):
    gp = SAMPLE_KV_PAGES
    k_pages = rest[:gp]
    v_pages = rest[gp:2 * gp]
    yb_o, m_scr, l_scr, acc_scr = rest[2 * gp:]
    c = pl.program_id(1)
    n_chunks = n_pages // gp
    head_of_lane = lax.broadcasted_iota(I32, (N_HEADS, ATT_WIDTH), 1) // HEAD_DIM
    head_of_row = lax.broadcasted_iota(I32, (N_HEADS, ATT_WIDTH), 0)
    own = head_of_lane == head_of_row
    q_blk = jnp.where(own, jnp.broadcast_to(q_ref[0], (N_HEADS, ATT_WIDTH)), 0.0)

    @pl.when(c == 0)
    def _():
        kn = kn_ref[0].astype(BF16).astype(F32)
        x_new = (jnp.sum(q_blk.astype(BF16).astype(F32) * kn, axis=1, keepdims=True)
                 + biasn_ref[...] + newm_ref[0][:, :1])
        m_scr[...] = x_new
        l_scr[...] = jnp.ones_like(l_scr)
        acc_scr[...] = jnp.broadcast_to(vn_ref[0].astype(BF16).astype(F32), (N_HEADS, ATT_WIDTH))

    q_bf = q_blk.astype(BF16)
    xs = []
    for g in range(gp):
        pg = c * gp + g
        kt = k_pages[g][0].astype(BF16)
        xs.append(_dot(q_bf, kt) + madd_ref[0, pg] + bias_ref[pg])
    x = jnp.concatenate(xs, axis=1)
    m = m_scr[...]
    m_new = jnp.maximum(m, jnp.max(x, axis=1, keepdims=True))
    alpha = jnp.exp2(m - m_new)
    p = jnp.exp2(x - m_new)
    l_scr[...] = alpha * l_scr[...] + jnp.sum(p, axis=1, keepdims=True)
    p_bf = p.astype(BF16)
    pv = jnp.zeros((N_HEADS, ATT_WIDTH), F32)
    for g in range(gp):
        vt = v_pages[g][0].astype(BF16)
        pv = pv + _dot_nt(p_bf[:, g * PAGE_SIZE:(g + 1) * PAGE_SIZE], vt)
    acc_scr[...] = alpha * acc_scr[...] + pv
    m_scr[...] = m_new

    @pl.when(c == n_chunks - 1)
    def _():
        o = acc_scr[...] / l_scr[...]
        yb_o[0] = jnp.sum(jnp.where(own, o, 0.0), axis=0, keepdims=True).astype(BF16)


def _attn_sample(page_table, q, k_new, v_new, madd, newm, bias_pages, bias_new, cache_kt, cache_vt):
    n, n_pages = page_table.shape
    gp = SAMPLE_KV_PAGES
    assert n_pages % gp == 0
    page_spec = lambda g: pl.BlockSpec((1, ATT_WIDTH, PAGE_SIZE), lambda b, c, pt: (pt[b, c * gp + g], 0, 0))
    per_seq = lambda shape: pl.BlockSpec((1,) + shape, lambda b, c, pt: (b, 0, 0))
    grid_spec = pltpu.PrefetchScalarGridSpec(
        num_scalar_prefetch=1,
        grid=(n, n_pages // gp),
        in_specs=[per_seq((1, ATT_WIDTH)), per_seq((1, ATT_WIDTH)), per_seq((1, ATT_WIDTH)),
                  pl.BlockSpec((1, n_pages, 1, PAGE_SIZE), lambda b, c, pt: (b, 0, 0, 0)), per_seq((1, PAGE_SIZE)),
                  pl.BlockSpec(bias_pages.shape, lambda b, c, pt: (0, 0, 0)),
                  pl.BlockSpec(bias_new.shape, lambda b, c, pt: (0, 0))]
        + [page_spec(g) for g in range(gp)] * 2,
        out_specs=per_seq((1, ATT_WIDTH)),
        scratch_shapes=[pltpu.VMEM((N_HEADS, 1), F32), pltpu.VMEM((N_HEADS, 1), F32),
                        pltpu.VMEM((N_HEADS, ATT_WIDTH), F32)],
    )
    return pl.pallas_call(
        functools.partial(_attn_sample_kernel, n_pages=n_pages),
        grid_spec=grid_spec,
        out_shape=jax.ShapeDtypeStruct((n, 1, ATT_WIDTH), BF16),
        compiler_params=_params(("parallel", "arbitrary")),
        name="attn_sample",
    )(page_table, q, k_new, v_new, madd, newm, bias_pages, bias_new, *([cache_kt] * gp), *([cache_vt] * gp))


def _merge_kernel(x_ref, ya_ref, yb_ref, sga_ref, sgb_ref, wa, wb, wo, g_ref, b_ref, x1_o, *, alpha):
    merged = (sga_ref[...].astype(F32) * _dot(ya_ref[...], wa[...])
              + sgb_ref[...].astype(F32) * _dot(yb_ref[...], wb[...]))
    z = alpha * x_ref[...] + _dot(merged.astype(BF16), wo[...])
    x1_o[...] = _layer_norm(z, g_ref[...], b_ref[...])


def _merge(x, ya, yb, sga, sgb, wa, wb, wo, g, b, alpha):
    n = x.shape[0]
    tm = min(TOKEN_TILE, n)
    rows = lambda c: pl.BlockSpec((tm, c), lambda i: (i, 0))
    return pl.pallas_call(
        functools.partial(_merge_kernel, alpha=alpha),
        grid=(n // tm,),
        in_specs=[rows(D_MODEL), rows(SSM_WIDTH), rows(ATT_WIDTH), rows(D_MODEL), rows(D_MODEL),
                  _const_spec(wa.shape), _const_spec(wb.shape), _const_spec(wo.shape),
                  _const_spec(g.shape), _const_spec(b.shape)],
        out_specs=rows(D_MODEL),
        out_shape=jax.ShapeDtypeStruct((n, D_MODEL), F32),
        compiler_params=_params(("parallel",)),
        name="merge_out_proj",
    )(x, ya, yb, sga, sgb, wa, wb, wo, g, b)


def _ffn_kernel(x1_ref, wg, wu, wd, g_ref, b_ref, y_o, acc_scr, *, alpha):
    x1 = x1_ref[...]
    xb = x1.astype(BF16)
    acc_scr[...] = alpha * x1
    for c in range(D_FF // FF_CHUNK):
        sl = slice(c * FF_CHUNK, (c + 1) * FF_CHUNK)
        gate = _dot(xb, wg[:, sl])
        up = _dot(xb, wu[:, sl])
        hid = (jax.nn.silu(gate) * up).astype(BF16)
        acc_scr[...] += _dot(hid, wd[sl, :])
    y_o[...] = _layer_norm(acc_scr[...], g_ref[...], b_ref[...])


def _ffn(x1, wg, wu, wd, g, b, alpha):
    n = x1.shape[0]
    tm = min(TOKEN_TILE, n)
    rows = pl.BlockSpec((tm, D_MODEL), lambda i: (i, 0))
    return pl.pallas_call(
        functools.partial(_ffn_kernel, alpha=alpha),
        grid=(n // tm,),
        in_specs=[rows, _const_spec(wg.shape), _const_spec(wu.shape), _const_spec(wd.shape),
                  _const_spec(g.shape), _const_spec(b.shape)],
        out_specs=rows,
        out_shape=jax.ShapeDtypeStruct((n, D_MODEL), F32),
        scratch_shapes=[pltpu.VMEM((tm, D_MODEL), F32)],
        compiler_params=_params(("parallel",)),
        name="ffn",
    )(x1, wg, wu, wd, g, b)


def _rel_bucket(rel):
    n = jnp.maximum(rel, 0)
    max_exact = REL_BUCKETS // 2
    nf = jnp.maximum(n, max_exact).astype(F32)
    large = max_exact + (jnp.log(nf / max_exact) / math.log(REL_MAX_DIST / max_exact)
                         * (REL_BUCKETS - max_exact)).astype(I32)
    large = jnp.minimum(large, REL_BUCKETS - 1)
    return jnp.where(n < max_exact, n, large)


def _bias_lookup(rel_bias, rel):
    onehot = (_rel_bucket(rel)[..., None] == jnp.arange(REL_BUCKETS, dtype=I32)).astype(F32)
    table = rel_bias.astype(F32) * LOG2E
    return jnp.einsum('...k,kh->...h', onehot, table, precision=lax.Precision.HIGHEST)


def _prompt_bias_tiles(rel_bias):
    kl = jnp.arange(KV_TILE, dtype=I32)[:, None]
    ql = jnp.arange(Q_TILE, dtype=I32)[None, :]
    rel = jnp.stack([ql - kl, ql - kl + KV_TILE])
    assert KV_TILE + 1 >= REL_MAX_DIST
    near = jnp.moveaxis(_bias_lookup(rel_bias, rel), -1, 0)
    far = _bias_lookup(rel_bias, jnp.full((), 2 * KV_TILE, I32))
    return near, far


def _ssm_operators(a_re, a_im, log_dt, b_re, b_im, c_re, c_im, d):
    g, p, gi = SSM_GROUPS, SSM_STATE, SSM_GROUP
    a_re, a_im = a_re.astype(F32), a_im.astype(F32)
    dt = jnp.exp(log_dt.astype(F32))[:, None]
    mag = jnp.exp(dt * a_re)
    lre = mag * jnp.cos(dt * a_im)
    lim = mag * jnp.sin(dt * a_im)
    den = a_re * a_re + a_im * a_im
    fr = ((lre - 1.0) * a_re + lim * a_im) / den
    fi = (lim * a_re - (lre - 1.0) * a_im) / den
    bbr = fr[:, :, None] * b_re - fi[:, :, None] * b_im
    bbi = fr[:, :, None] * b_im + fi[:, :, None] * b_re
    eye = jnp.eye(g, dtype=F32)
    blockdiag_in = lambda m: jnp.einsum('gpi,gh->gihp', m, eye).reshape(g * gi, g * p)
    blockdiag_out = lambda m: jnp.einsum('gip,gh->gphi', m, eye).reshape(g * p, g * gi)
    bfull = jnp.concatenate([blockdiag_in(bbr), blockdiag_in(bbi)], axis=1)
    cfull = jnp.concatenate([blockdiag_out(c_re.astype(F32)), blockdiag_out(-c_im.astype(F32))], axis=0)
    return {'lre': lre.reshape(1, g * p), 'lim': lim.reshape(1, g * p),
            'bfull': bfull.astype(BF16), 'cfull': cfull.astype(BF16), 'd': d.astype(F32).reshape(1, g * gi)}


def _split_w_in(w_in):
    offs = [0]
    for n in IN_SPLITS:
        offs.append(offs[-1] + n)
    names = ['u', 'q', 'k', 'v', 'qi', 'ki', 'wi', 'ga', 'gb']
    return {nm: w_in[:, offs[j]:offs[j + 1]] for j, nm in enumerate(names)}


def kernel(x_prompt, x_sample, cache_k, cache_v, cache_idx_k, state_ssm_re, state_ssm_im, page_table, w_in, ssm_a_re, ssm_a_im, ssm_log_dt, ssm_b_re, ssm_b_im, ssm_c_re, ssm_c_im, ssm_d, w_glu, w_branch_a, w_branch_b, rel_bias, w_out, ln1_g, ln1_b, w_ffn_in, w_ffn_out, ln2_g, ln2_b):
    depth = w_in.shape[0]
    bsz, s_len, _ = x_prompt.shape
    n_dec, dec_seq, _ = x_sample.shape
    assert dec_seq == 1
    n_pages = page_table.shape[1]
    n_pool = cache_k.shape[1]
    past = n_pages * PAGE_SIZE
    alpha = (2 * depth) ** 0.25
    idx_scale = IDX_DIM ** -0.5 * IDX_HEADS ** -0.5
    att_scale = HEAD_DIM ** -0.5 * LOG2E

    bias_t, bias_far = _prompt_bias_tiles(rel_bias)
    key_pos = jnp.arange(past, dtype=I32).reshape(n_pages, PAGE_SIZE)
    bias_pages = jnp.swapaxes(_bias_lookup(rel_bias, past - key_pos), 1, 2)
    bias_new = _bias_lookup(rel_bias, jnp.zeros((), I32)).reshape(N_HEADS, 1)

    x_p = x_prompt
    x_s = x_sample.reshape(n_dec, D_MODEL)
    outs = {k: [] for k in ('kp', 'vp', 'kip', 'hpr', 'hpi', 'ks', 'vs', 'kis', 'hsr', 'hsi')}
    for l in range(depth):
        w = _split_w_in(w_in[l])
        wn = {nm: w[nm].astype(BF16) for nm in ('u', 'k', 'v', 'ki', 'ga', 'gb')}
        wn['tq'] = (w['q'] * att_scale).T.astype(BF16)
        wn['tv'] = w['v'].T.astype(BF16)
        wn['tqi'] = w['qi'].T.astype(BF16)
        wn['twi'] = jnp.pad((w['wi'] * idx_scale).T, ((0, 16 - IDX_HEADS), (0, 0))).astype(BF16)
        ws = {nm: w[nm].astype(BF16) for nm in ('u', 'k', 'v', 'qi', 'ki', 'ga', 'gb')}
        ws['q'] = (w['q'] * att_scale).astype(BF16)
        ws['wi'] = (w['wi'] * idx_scale).astype(BF16)
        ops = _ssm_operators(ssm_a_re[l], ssm_a_im[l], ssm_log_dt[l], ssm_b_re[l], ssm_b_im[l],
                             ssm_c_re[l], ssm_c_im[l], ssm_d[l])
        wglu = w_glu[l].astype(BF16)
        wa, wb, wo = w_branch_a[l].astype(BF16), w_branch_b[l].astype(BF16), w_out[l].astype(BF16)
        wg, wu = w_ffn_in[l][:, :D_FF].astype(BF16), w_ffn_in[l][:, D_FF:].astype(BF16)
        wd = w_ffn_out[l].astype(BF16)
        g1, b1 = ln1_g[l].astype(F32).reshape(1, D_MODEL), ln1_b[l].astype(F32).reshape(1, D_MODEL)
        g2, b2 = ln2_g[l].astype(F32).reshape(1, D_MODEL), ln2_b[l].astype(F32).reshape(1, D_MODEL)

        u, k, v, ki, k_bf, ki_bf, sga, sgb, qt, vt, qit, wit = _in_proj_prompt(x_p, wn)
        ya, hpr, hpi = _ssm_prompt(u, ops, wglu)
        yb = _attn_prompt(qt, qit, wit, k_bf, vt, ki_bf, bias_t, bias_far)
        n_tok = bsz * s_len
        flat = lambda a: a.reshape(n_tok, a.shape[-1])
        x1 = _merge(flat(x_p), flat(ya), flat(yb), flat(sga), flat(sgb), wa, wb, wo, g1, b1, alpha)
        x_p = _ffn(x1, wg, wu, wd, g2, b2, alpha).reshape(bsz, s_len, D_MODEL)
        outs['kp'].append(k.reshape(bsz, s_len, N_HEADS, HEAD_DIM))
        outs['vp'].append(v.reshape(bsz, s_len, N_HEADS, HEAD_DIM))
        outs['kip'].append(ki)
        outs['hpr'].append(hpr.reshape(bsz, SSM_GROUPS, SSM_STATE))
        outs['hpi'].append(hpi.reshape(bsz, SSM_GROUPS, SSM_STATE))

        us, qs, k_s, v_s, qis, kis, wis, sga_s, sgb_s = _in_proj_sample(x_s, ws)
        ya_s, hsr, hsi = _ssm_sample(us, state_ssm_re[l].reshape(n_dec, SSM_COLS).astype(F32),
                                     state_ssm_im[l].reshape(n_dec, SSM_COLS).astype(F32), ops, wglu)
        cache_kt = jnp.transpose(cache_k[l], (0, 2, 3, 1)).reshape(n_pool, ATT_WIDTH, PAGE_SIZE)
        cache_vt = jnp.transpose(cache_v[l], (0, 2, 3, 1)).reshape(n_pool, ATT_WIDTH, PAGE_SIZE)
        cache_it = jnp.transpose(cache_idx_k[l], (0, 2, 1))
        keys, newkey = _sample_scores(
            page_table, qis.astype(BF16).reshape(n_dec, IDX_HEADS, IDX_DIM), wis.reshape(n_dec, IDX_HEADS, 1),
            kis.astype(BF16).reshape(n_dec, 1, IDX_DIM), cache_it)
        madd, newm = _sample_select(keys.reshape(n_dec, past), newkey.reshape(n_dec, PAGE_SIZE))
        yb_s = _attn_sample(page_table, qs.reshape(n_dec, 1, ATT_WIDTH), k_s.reshape(n_dec, 1, ATT_WIDTH),
                            v_s.reshape(n_dec, 1, ATT_WIDTH), madd.reshape(n_dec, n_pages, 1, PAGE_SIZE),
                            newm.reshape(n_dec, 1, PAGE_SIZE), bias_pages, bias_new, cache_kt, cache_vt)
        x1_s = _merge(x_s, ya_s, yb_s.reshape(n_dec, ATT_WIDTH), sga_s, sgb_s, wa, wb, wo, g1, b1, alpha)
        x_s = _ffn(x1_s, wg, wu, wd, g2, b2, alpha)
        outs['ks'].append(k_s.reshape(n_dec, 1, N_HEADS, HEAD_DIM))
        outs['vs'].append(v_s.reshape(n_dec, 1, N_HEADS, HEAD_DIM))
        outs['kis'].append(kis.reshape(n_dec, 1, IDX_DIM))
        outs['hsr'].append(hsr.reshape(n_dec, SSM_GROUPS, SSM_STATE))
        outs['hsi'].append(hsi.reshape(n_dec, SSM_GROUPS, SSM_STATE))

    st = lambda name: jnp.stack(outs[name])
    return (x_p, x_s.reshape(n_dec, 1, D_MODEL),
            st('kp'), st('vp'), st('kip'), st('hpr'), st('hpi'),
            st('ks'), st('vs'), st('kis'), st('hsr'), st('hsi'))
```

```python
import functools
import math

import jax
import jax.numpy as jnp
from jax import lax
from jax.experimental import pallas as pl
from jax.experimental.pallas import tpu as pltpu

F32 = jnp.float32
BF16 = jnp.bfloat16
I32 = jnp.int32

D_MODEL = 1024
SSM_WIDTH = 512
SSM_GROUP = 16
SSM_GROUPS = 32
SSM_STATE = 64
SSM_COLS = SSM_GROUPS * SSM_STATE
N_HEADS = 8
HEAD_DIM = 64
ATT_WIDTH = 512
IDX_HEADS = 8
IDX_DIM = 32
TOPK_MAX = 256
PAGE_SIZE = 128
REL_BUCKETS = 32
REL_MAX_DIST = 128
D_FF = 2816
LN_EPS = 1e-5
IN_SPLITS = (SSM_WIDTH, ATT_WIDTH, ATT_WIDTH, ATT_WIDTH, IDX_HEADS * IDX_DIM, IDX_DIM, IDX_HEADS, D_MODEL, D_MODEL)
LOG2E = math.log2(math.e)

VMEM_LIMIT_BYTES = 56 * 1024 * 1024
NEG_BIG = -1e30
KEY_NEG_INF = -2139095041
KEY_MAX = 2147483647

TOKEN_TILE = 512
Q_TILE = 256
KV_TILE = 256
SSM_T_TILE = 64
FF_CHUNK = 256
SAMPLE_IDX_PAGES = 32
SAMPLE_KV_PAGES = 8


def _params(sem):
    return pltpu.CompilerParams(dimension_semantics=sem, vmem_limit_bytes=VMEM_LIMIT_BYTES)


def _const_spec(shape):
    n = len(shape)
    return pl.BlockSpec(shape, lambda *_: (0,) * n)


def _resident_spec(block, index_map):
    return pl.BlockSpec(block, index_map, pipeline_mode=pl.Buffered(1))


def _dot(a, b):
    return jnp.dot(a, b, preferred_element_type=F32)


def _dot_nt(a, b):
    return lax.dot_general(a, b, (((1,), (1,)), ((), ())), preferred_element_type=F32)


def _layer_norm(z, g, b):
    mu = jnp.mean(z, axis=-1, keepdims=True)
    zc = z - mu
    var = jnp.mean(zc * zc, axis=-1, keepdims=True)
    return zc * lax.rsqrt(var + LN_EPS) * g + b


def _ordered_key(x):
    b = pltpu.bitcast(x, I32)
    return b ^ ((b >> 31) & KEY_MAX)


def _in_proj_prompt_kernel(x_ref, wu, wk, wv, wga, wgb, wki, wtq, wtv, wtqi, wtwi,
                           u_o, k_o, v_o, ki_o, kbf_o, kibf_o, sga_o, sgb_o, qt_o, vt_o, qit_o, wit_o):
    xb = x_ref[0].astype(BF16)
    u_o[0] = _dot(xb, wu[...])
    k = _dot(xb, wk[...])
    k_o[0] = k
    kbf_o[0] = k.astype(BF16)
    v_o[0] = _dot(xb, wv[...])
    ki = _dot(xb, wki[...])
    ki_o[0] = ki
    kibf_o[0] = ki.astype(BF16)
    sga_o[0] = jax.nn.sigmoid(_dot(xb, wga[...])).astype(BF16)
    sgb_o[0] = jax.nn.sigmoid(_dot(xb, wgb[...])).astype(BF16)
    qt_o[0] = _dot_nt(wtq[...], xb).astype(BF16)
    vt_o[0] = _dot_nt(wtv[...], xb).astype(BF16)
    qit_o[0] = _dot_nt(wtqi[...], xb).astype(BF16)
    wit_o[0] = _dot_nt(wtwi[...], xb)[:IDX_HEADS]


def _in_proj_prompt(x, w):
    bsz, s_len, _ = x.shape
    tm = min(TOKEN_TILE, s_len)
    nat = lambda c: pl.BlockSpec((1, tm, c), lambda b, i: (b, i, 0))
    tr = lambda c: pl.BlockSpec((1, c, tm), lambda b, i: (b, 0, i))
    sds = jax.ShapeDtypeStruct
    weights = [w['u'], w['k'], w['v'], w['ga'], w['gb'], w['ki'], w['tq'], w['tv'], w['tqi'], w['twi']]
    return pl.pallas_call(
        _in_proj_prompt_kernel,
        grid=(bsz, s_len // tm),
        in_specs=[nat(D_MODEL)] + [_const_spec(a.shape) for a in weights],
        out_specs=[nat(SSM_WIDTH), nat(ATT_WIDTH), nat(ATT_WIDTH), nat(IDX_DIM), nat(ATT_WIDTH), nat(IDX_DIM),
                   nat(D_MODEL), nat(D_MODEL), tr(ATT_WIDTH), tr(ATT_WIDTH), tr(IDX_HEADS * IDX_DIM), tr(IDX_HEADS)],
        out_shape=[sds((bsz, s_len, SSM_WIDTH), F32), sds((bsz, s_len, ATT_WIDTH), F32),
                   sds((bsz, s_len, ATT_WIDTH), F32), sds((bsz, s_len, IDX_DIM), F32),
                   sds((bsz, s_len, ATT_WIDTH), BF16), sds((bsz, s_len, IDX_DIM), BF16),
                   sds((bsz, s_len, D_MODEL), BF16), sds((bsz, s_len, D_MODEL), BF16),
                   sds((bsz, ATT_WIDTH, s_len), BF16), sds((bsz, ATT_WIDTH, s_len), BF16),
                   sds((bsz, IDX_HEADS * IDX_DIM, s_len), BF16), sds((bsz, IDX_HEADS, s_len), F32)],
        compiler_params=_params(("parallel", "parallel")),
        name="in_proj_prompt",
    )(x, *weights)


def _in_proj_sample_kernel(x_ref, wu, wq, wk, wv, wqi, wki, wwi, wga, wgb,
                           u_o, q_o, k_o, v_o, qi_o, ki_o, wi_o, sga_o, sgb_o):
    xb = x_ref[...].astype(BF16)
    u_o[...] = _dot(xb, wu[...])
    q_o[...] = _dot(xb, wq[...])
    k_o[...] = _dot(xb, wk[...])
    v_o[...] = _dot(xb, wv[...])
    qi_o[...] = _dot(xb, wqi[...])
    ki_o[...] = _dot(xb, wki[...])
    wi_o[...] = _dot(xb, wwi[...])
    sga_o[...] = jax.nn.sigmoid(_dot(xb, wga[...])).astype(BF16)
    sgb_o[...] = jax.nn.sigmoid(_dot(xb, wgb[...])).astype(BF16)


def _in_proj_sample(x, w):
    n = x.shape[0]
    weights = [w['u'], w['q'], w['k'], w['v'], w['qi'], w['ki'], w['wi'], w['ga'], w['gb']]
    cols = [SSM_WIDTH, ATT_WIDTH, ATT_WIDTH, ATT_WIDTH, IDX_HEADS * IDX_DIM, IDX_DIM, IDX_HEADS, D_MODEL, D_MODEL]
    dts = [F32] * 7 + [BF16, BF16]
    return pl.pallas_call(
        _in_proj_sample_kernel,
        grid=(1,),
        in_specs=[_const_spec(x.shape)] + [_const_spec(a.shape) for a in weights],
        out_specs=[_const_spec((n, c)) for c in cols],
        out_shape=[jax.ShapeDtypeStruct((n, c), d) for c, d in zip(cols, dts)],
        compiler_params=_params(("arbitrary",)),
        name="in_proj_sample",
    )(x, *weights)


def _glu_tail(h_bf, u, cfull, dvec, wglu):
    y = _dot(h_bf, cfull[...]) + dvec[...] * u
    y = jax.nn.gelu(y)
    return y * jax.nn.sigmoid(_dot(y.astype(BF16), wglu[...]))


def _ssm_prompt_kernel(u_ref, perm_ref, permt_ref, bfull, cfull, lre_ref, lim_ref, dvec, wglu,
                       ya_o, hre_o, him_o, h_scr, u_scr, y_scr, state_scr, *, bsz, tt):
    c = pl.program_id(0)
    rows = bsz * tt
    row_chunk = 256

    @pl.when(c == 0)
    def _():
        state_scr[...] = jnp.zeros_like(state_scr)

    u_bt = jnp.concatenate([u_ref[b] for b in range(bsz)], axis=0)
    u_hi = u_bt.astype(BF16)
    u_lo = (u_bt - u_hi.astype(F32)).astype(BF16)
    u_scr[...] = _dot(perm_ref[...], u_hi) + _dot(perm_ref[...], u_lo)
    for r in range(rows // row_chunk):
        sl = slice(r * row_chunk, (r + 1) * row_chunk)
        h_scr[sl, :] = _dot(u_scr[sl, :].astype(BF16), bfull[...])

    chunk = 512
    for ci in range(SSM_COLS // chunk):
        re_sl = slice(ci * chunk, (ci + 1) * chunk)
        im_sl = slice(SSM_COLS + ci * chunk, SSM_COLS + (ci + 1) * chunk)
        lre = lre_ref[:, re_sl]
        lim = lim_ref[:, re_sl]

        def step(t, carry):
            hr, hi = carry
            r0 = pl.multiple_of(t * bsz, bsz)
            nr = lre * hr - lim * hi + h_scr[pl.ds(r0, bsz), re_sl]
            ni = lre * hi + lim * hr + h_scr[pl.ds(r0, bsz), im_sl]
            h_scr[pl.ds(r0, bsz), re_sl] = nr
            h_scr[pl.ds(r0, bsz), im_sl] = ni
            return nr, ni

        hr, hi = lax.fori_loop(0, tt, step, (state_scr[:, re_sl], state_scr[:, im_sl]), unroll=4)
        state_scr[:, re_sl] = hr
        state_scr[:, im_sl] = hi

    for r in range(rows // row_chunk):
        sl = slice(r * row_chunk, (r + 1) * row_chunk)
        y_scr[sl, :] = _glu_tail(h_scr[sl, :].astype(BF16), u_scr[sl, :], cfull, dvec, wglu).astype(BF16)

    ya_bt = _dot(permt_ref[...], y_scr[...]).astype(BF16)
    for b in range(bsz):
        ya_o[b] = ya_bt[b * tt:(b + 1) * tt]

    @pl.when(c == pl.num_programs(0) - 1)
    def _():
        hre_o[...] = state_scr[:, :SSM_COLS]
        him_o[...] = state_scr[:, SSM_COLS:]


def _ssm_prompt(u, ops, wglu):
    bsz, s_len, _ = u.shape
    tt = min(SSM_T_TILE, s_len)
    rows = bsz * tt
    lre = jnp.broadcast_to(ops['lre'], (bsz, SSM_COLS))
    lim = jnp.broadcast_to(ops['lim'], (bsz, SSM_COLS))
    kern = functools.partial(_ssm_prompt_kernel, bsz=bsz, tt=tt)
    src = (jnp.arange(rows, dtype=I32) % bsz) * tt + jnp.arange(rows, dtype=I32) // bsz
    perm = (src[:, None] == jnp.arange(rows, dtype=I32)[None, :]).astype(BF16)
    return pl.pallas_call(
        kern,
        grid=(s_len // tt,),
        in_specs=[pl.BlockSpec((bsz, tt, SSM_WIDTH), lambda c: (0, c, 0)),
                  _const_spec(perm.shape), _const_spec(perm.shape),
                  _const_spec(ops['bfull'].shape), _const_spec(ops['cfull'].shape),
                  _const_spec(lre.shape), _const_spec(lim.shape), _const_spec(ops['d'].shape),
                  _const_spec(wglu.shape)],
        out_specs=[pl.BlockSpec((bsz, tt, SSM_WIDTH), lambda c: (0, c, 0)),
                   _const_spec((bsz, SSM_COLS)), _const_spec((bsz, SSM_COLS))],
        out_shape=[jax.ShapeDtypeStruct((bsz, s_len, SSM_WIDTH), BF16),
                   jax.ShapeDtypeStruct((bsz, SSM_COLS), F32), jax.ShapeDtypeStruct((bsz, SSM_COLS), F32)],
        scratch_shapes=[pltpu.VMEM((rows, 2 * SSM_COLS), F32), pltpu.VMEM((rows, SSM_WIDTH), F32),
                        pltpu.VMEM((rows, SSM_WIDTH), BF16), pltpu.VMEM((bsz, 2 * SSM_COLS), F32)],
        compiler_params=_params(("arbitrary",)),
        name="ssm_prompt",
    )(u, perm, perm.T, ops['bfull'], ops['cfull'], lre, lim, ops['d'], wglu)


def _ssm_sample_kernel(u_ref, h0re_ref, h0im_ref, bfull, cfull, lre_ref, lim_ref, dvec, wglu,
                       ya_o, hre_o, him_o):
    u = u_ref[...]
    bu = _dot(u.astype(BF16), bfull[...])
    lre = lre_ref[...]
    lim = lim_ref[...]
    h0r = h0re_ref[...]
    h0i = h0im_ref[...]
    hr = lre * h0r - lim * h0i + bu[:, :SSM_COLS]
    hi = lre * h0i + lim * h0r + bu[:, SSM_COLS:]
    hre_o[...] = hr
    him_o[...] = hi
    h_bf = jnp.concatenate([hr.astype(BF16), hi.astype(BF16)], axis=1)
    ya_o[...] = _glu_tail(h_bf, u, cfull, dvec, wglu).astype(BF16)


def _ssm_sample(u, h0re, h0im, ops, wglu):
    n = u.shape[0]
    args = [u, h0re, h0im, ops['bfull'], ops['cfull'], ops['lre'], ops['lim'], ops['d'], wglu]
    return pl.pallas_call(
        _ssm_sample_kernel,
        grid=(1,),
        in_specs=[_const_spec(a.shape) for a in args],
        out_specs=[_const_spec((n, SSM_WIDTH)), _const_spec((n, SSM_COLS)), _const_spec((n, SSM_COLS))],
        out_shape=[jax.ShapeDtypeStruct((n, SSM_WIDTH), BF16),
                   jax.ShapeDtypeStruct((n, SSM_COLS), F32), jax.ShapeDtypeStruct((n, SSM_COLS), F32)],
        compiler_params=_params(("arbitrary",)),
        name="ssm_sample",
    )(*args)


def _bisect_kth(count_ge, lo, hi, c_lo, c_hi, k):
    def cond(st):
        it, lo, hi, _, _ = st
        return jnp.logical_and(it < 34, jnp.max((lo + 1 != hi).astype(I32)) > 0)

    def body(st):
        it, lo, hi, c_lo, c_hi = st
        mid = (lo >> 1) + (hi >> 1) + (lo & hi & 1)
        cnt = count_ge(mid)
        active = lo + 1 != hi
        up = jnp.logical_and(active, cnt >= k)
        down = jnp.logical_and(active, cnt < k)
        n_lo = jnp.where(up, mid, lo)
        n_clo = jnp.where(up, cnt, c_lo)
        n_hi = jnp.where(down, mid, jnp.where(jnp.logical_and(active, cnt == k), mid + 1, hi))
        n_chi = jnp.where(down, cnt, c_hi)
        return it + 1, n_lo, n_hi, n_clo, n_chi

    _, lo, hi, c_lo, c_hi = lax.while_loop(cond, body, (jnp.int32(0), lo, hi, c_lo, c_hi))
    return lo, c_lo, c_hi


def _bisect_tie_index(count_tie_le, need, n_tie_rows, idx_hi, n_bits):
    def cond(st):
        it, _, _ = st
        return jnp.logical_and(it < n_bits, n_tie_rows > 0)

    def body(st):
        it, lo, hi = st
        mid = (lo + hi) >> 1
        ok = count_tie_le(mid) >= need
        return it + 1, jnp.where(ok, lo, mid), jnp.where(ok, mid, hi)

    lo0 = jnp.full_like(need, -1)
    hi0 = jnp.full_like(need, idx_hi)
    _, _, hi = lax.while_loop(cond, body, (jnp.int32(0), lo0, hi0))
    return hi


def _attn_prompt_kernel(qt_ref, qit_ref, wit_ref, k_ref, vt_ref, ki_ref, bias_ref, bfar_ref, yb_o,
                        s_scr, s16_scr, o_scr, qm_scr, m_scr, l_scr, a_scr, off_scr, x_scr, *, k_sel):
    i = pl.program_id(1)
    n_tiles = i + 1
    tq = Q_TILE
    tk = KV_TILE
    q_pos = i * tq + lax.broadcasted_iota(I32, (1, tq), 1)
    kv_iota = lax.broadcasted_iota(I32, (tk, tq), 0)

    n_pairs = (n_tiles + 1) // 2

    def score_pair(jj, top):
        for half in range(2):
            r0 = pl.multiple_of((2 * jj + half) * tk, tk)
            ki_t = ki_ref[0, pl.ds(r0, tk), :]
            acc = jnp.zeros((tk, tq), F32)
            for h in range(IDX_HEADS):
                s = _dot(ki_t, qit_ref[0, h * IDX_DIM:(h + 1) * IDX_DIM, :])
                acc = acc + wit_ref[0, h:h + 1, :] * jnp.maximum(s, 0.0)
            acc = jnp.where(kv_iota + r0 <= q_pos, acc, -jnp.inf)
            key = _ordered_key(acc)
            s_scr[pl.ds(r0, tk), :] = key
            s16_scr[pl.ds(r0, tk), :] = (key >> 16).astype(jnp.int16)
            top = jnp.maximum(top, jnp.max(acc, axis=0, keepdims=True))
        return top

    top = lax.fori_loop(0, n_pairs, score_pair, jnp.full((1, tq), -jnp.inf, F32))

    def count_tiles(pred):
        def tile(jj, acc):
            for half in range(2):
                r0 = pl.multiple_of((2 * jj + half) * tk, tk)
                ind = pred(s_scr[pl.ds(r0, tk), :], r0).astype(I32)
                for r in range(tk // 8):
                    acc = acc + ind[r * 8:(r + 1) * 8]
            return acc
        acc = lax.fori_loop(0, n_pairs, tile, jnp.zeros((8, tq), I32))
        return jnp.sum(acc, axis=0, keepdims=True)

    def count_ge_high(mid):
        mid16 = mid.astype(jnp.int16)

        def tile(jj, acc):
            r0 = pl.multiple_of(jj * 2 * tk, 2 * tk)
            ind = jnp.where(s16_scr[pl.ds(r0, 2 * tk), :] >= mid16, jnp.int16(1), jnp.int16(0))
            for r in range(2 * tk // 16):
                acc = acc + ind[r * 16:(r + 1) * 16]
            return acc
        acc = lax.fori_loop(0, n_pairs, tile, jnp.zeros((16, tq), jnp.int16))
        return jnp.sum(acc.astype(I32), axis=0, keepdims=True)

    count_ge = lambda mid: count_tiles(lambda t, r0: t >= mid)
    n_valid = q_pos + 1
    searching = n_valid > k_sel
    zeros = jnp.zeros((1, tq), I32)
    floor_high = jnp.full((1, tq), (KEY_NEG_INF + 1) >> 16, I32)
    top_high = _ordered_key(top) >> 16
    guess = jnp.maximum(top_high - (4 << 7), 0)
    c_guess = count_ge_high(guess)
    use_guess = jnp.logical_and(jnp.logical_and(searching, top_high > 0), c_guess >= k_sel)
    lo_high = jnp.where(use_guess, guess, floor_high)
    c_lo_high = jnp.where(use_guess, c_guess, n_valid)
    hi_high = jnp.where(searching, top_high + 1, floor_high + 1)
    bucket, c_bucket, c_next = _bisect_kth(count_ge_high, lo_high, hi_high, c_lo_high, zeros, k_sel)
    settled = jnp.logical_or(jnp.logical_not(searching), c_bucket == k_sel)
    lo0 = bucket << 16
    bucket_end = jnp.where(bucket >= 32767, KEY_MAX, (bucket + 1) << 16)
    hi0 = jnp.where(settled, lo0 + 1, bucket_end)
    thr, c_thr, c_above = _bisect_kth(count_ge, lo0, hi0, c_bucket, c_next, k_sel)
    has_tie = jnp.logical_and(searching, c_thr > k_sel)
    need = k_sel - c_above
    tie_j = _bisect_tie_index(
        lambda mid: count_tiles(lambda t, r0: jnp.logical_and(t == thr, kv_iota + r0 <= mid)),
        need, jnp.max(has_tie.astype(I32)), n_tiles * tk - 1, 14)
    tie_j = jnp.where(has_tie, tie_j, KEY_MAX)

    def mask_pair(jj, _):
        for half in range(2):
            r0 = pl.multiple_of((2 * jj + half) * tk, tk)
            t = s_scr[pl.ds(r0, tk), :]
            tie_ok = jnp.where(kv_iota + r0 <= tie_j, 0.0, NEG_BIG)
            madd = jnp.where(t > thr, 0.0, jnp.where(t == thr, tie_ok, NEG_BIG))
            s_scr[pl.ds(r0, tk), :] = pltpu.bitcast(madd.astype(F32), I32)
        return 0

    lax.fori_loop(0, n_pairs, mask_pair, 0)

    row = lax.broadcasted_iota(I32, (2 * HEAD_DIM, tq), 0)
    for pair in range(N_HEADS // 2):
        q_pair = qt_ref[0, pair * 128:(pair + 1) * 128, :]
        zero = jnp.zeros_like(q_pair)
        qm_scr[2 * pair] = jnp.where(row < HEAD_DIM, q_pair, zero)
        qm_scr[2 * pair + 1] = jnp.where(row >= HEAD_DIM, q_pair, zero)
    m_scr[...] = jnp.full(m_scr.shape, NEG_BIG, F32)
    l_scr[...] = jnp.zeros(l_scr.shape, F32)
    o_scr[...] = jnp.zeros(o_scr.shape, F32)

    def attn_tiles(tiles):
        for slot, (j, near_kind) in enumerate(tiles):
            r0 = pl.multiple_of(j * tk, tk)
            for pair in range(N_HEADS // 2):
                k_t = k_ref[0, pl.ds(r0, tk), pair * 128:(pair + 1) * 128]
                for h in (2 * pair, 2 * pair + 1):
                    x = _dot(k_t, qm_scr[h]) + pltpu.bitcast(s_scr[pl.ds(r0, tk), :], F32)
                    if near_kind is None:
                        c = bfar_ref[h]
                    else:
                        x = x + bias_ref[h, near_kind]
                        c = 0.0
                    x_scr[slot, h] = x
                    m = m_scr[h]
                    m_new = jnp.maximum(m, jnp.max(x, axis=0, keepdims=True) + c)
                    a_scr[slot, h] = jnp.exp2(m - m_new)
                    off_scr[slot, h] = m_new - c
                    m_scr[h] = m_new
        for slot, (j, _) in enumerate(tiles):
            r0 = pl.multiple_of(j * tk, tk)
            for h in range(N_HEADS):
                alpha = a_scr[slot, h]
                p = jnp.exp2(x_scr[slot, h] - off_scr[slot, h])
                l_scr[h] = alpha * l_scr[h] + jnp.sum(p, axis=0, keepdims=True)
                v_t = vt_ref[0, h * HEAD_DIM:(h + 1) * HEAD_DIM, pl.ds(r0, tk)]
                hs = slice(h * HEAD_DIM, (h + 1) * HEAD_DIM)
                o_scr[hs, :] = alpha * o_scr[hs, :] + _dot(v_t, p.astype(BF16))

    n_far = jnp.maximum(i - 1, 0)

    def far_pair(jj, _):
        attn_tiles([(2 * jj, None), (2 * jj + 1, None)])
        return 0

    lax.fori_loop(0, n_far // 2, far_pair, 0)

    @pl.when(n_far % 2 == 1)
    def _():
        attn_tiles([(n_far - 1, None)])

    @pl.when(i >= 1)
    def _():
        attn_tiles([(i - 1, 1), (i, 0)])

    @pl.when(i == 0)
    def _():
        attn_tiles([(i, 0)])

    for h in range(N_HEADS):
        hs = slice(h * HEAD_DIM, (h + 1) * HEAD_DIM)
        o_scr[hs, :] = o_scr[hs, :] / l_scr[h]
    yb_o[0] = o_scr[...].T.astype(BF16)


def _attn_prompt(qt, qit, wit, k_bf, vt, ki_bf, bias_t, bias_far):
    bsz, _, s_len = qt.shape
    assert s_len % (2 * KV_TILE) == 0 and Q_TILE == KV_TILE
    qblk = lambda c: pl.BlockSpec((1, c, Q_TILE), lambda b, i: (b, 0, i))
    return pl.pallas_call(
        functools.partial(_attn_prompt_kernel, k_sel=min(TOPK_MAX, s_len // 4)),
        grid=(bsz, s_len // Q_TILE),
        in_specs=[qblk(ATT_WIDTH), qblk(IDX_HEADS * IDX_DIM), qblk(IDX_HEADS),
                  _resident_spec((1, s_len, ATT_WIDTH), lambda b, i: (b, 0, 0)),
                  _resident_spec((1, ATT_WIDTH, s_len), lambda b, i: (b, 0, 0)),
                  _resident_spec((1, s_len, IDX_DIM), lambda b, i: (b, 0, 0)),
                  _resident_spec(bias_t.shape, lambda b, i: (0, 0, 0, 0)),
                  pl.BlockSpec(memory_space=pltpu.SMEM)],
        out_specs=pl.BlockSpec((1, Q_TILE, ATT_WIDTH), lambda b, i: (b, i, 0)),
        out_shape=jax.ShapeDtypeStruct((bsz, s_len, ATT_WIDTH), BF16),
        scratch_shapes=[pltpu.VMEM((s_len, Q_TILE), I32), pltpu.VMEM((s_len, Q_TILE), jnp.int16),
                        pltpu.VMEM((ATT_WIDTH, Q_TILE), F32),
                        pltpu.VMEM((N_HEADS, 2 * HEAD_DIM, Q_TILE), BF16),
                        pltpu.VMEM((N_HEADS, 1, Q_TILE), F32), pltpu.VMEM((N_HEADS, 1, Q_TILE), F32),
                        pltpu.VMEM((2, N_HEADS, 1, Q_TILE), F32), pltpu.VMEM((2, N_HEADS, 1, Q_TILE), F32),
                        pltpu.VMEM((2, N_HEADS, KV_TILE, Q_TILE), F32)],
        compiler_params=_params(("parallel", "arbitrary")),
        name="attn_prompt",
    )(qt, qit, wit, k_bf, vt, ki_bf, bias_t, bias_far)


def _sample_scores_kernel(pt_ref, qi_ref, wi_ref, kin_ref, *rest, gp):
    pages = rest[:gp]
    keys_o, newkey_o = rest[gp:]
    qi = qi_ref[0]
    wi = wi_ref[0]
    rows = []
    for g in range(gp):
        s = _dot(qi, pages[g][0].astype(BF16))
        rows.append(jnp.sum(wi * jnp.maximum(s, 0.0), axis=0, keepdims=True))
    keys_o[0] = _ordered_key(jnp.concatenate(rows, axis=0))

    @pl.when(pl.program_id(1) == 0)
    def _():
        s_new = jnp.sum(qi.astype(F32) * kin_ref[0].astype(F32), axis=1, keepdims=True)
        key_new = _ordered_key(jnp.sum(wi * jnp.maximum(s_new, 0.0), axis=0, keepdims=True))
        newkey_o[0] = jnp.broadcast_to(key_new, (1, PAGE_SIZE))


def _sample_scores(page_table, qi, wi, ki_new, cache_idx_t):
    n, n_pages = page_table.shape
    gp = min(SAMPLE_IDX_PAGES, n_pages)
    assert n_pages % gp == 0
    page_spec = lambda g: pl.BlockSpec((1, IDX_DIM, PAGE_SIZE), lambda b, c, pt: (pt[b, c * gp + g], 0, 0))
    per_seq = lambda shape: pl.BlockSpec((1,) + shape, lambda b, c, pt: (b, 0, 0))
    grid_spec = pltpu.PrefetchScalarGridSpec(
        num_scalar_prefetch=1,
        grid=(n, n_pages // gp),
        in_specs=[per_seq((IDX_HEADS, IDX_DIM)), per_seq((IDX_HEADS, 1)), per_seq((1, IDX_DIM))]
        + [page_spec(g) for g in range(gp)],
        out_specs=[pl.BlockSpec((1, gp, PAGE_SIZE), lambda b, c, pt: (b, c, 0)), per_seq((1, PAGE_SIZE))],
    )
    return pl.pallas_call(
        functools.partial(_sample_scores_kernel, gp=gp),
        grid_spec=grid_spec,
        out_shape=[jax.ShapeDtypeStruct((n, n_pages, PAGE_SIZE), I32), jax.ShapeDtypeStruct((n, 1, PAGE_SIZE), I32)],
        compiler_params=_params(("parallel", "arbitrary")),
        name="sample_scores",
    )(page_table, qi, wi, ki_new, *([cache_idx_t] * gp))


def _sample_select_kernel(keys_ref, newkey_ref, madd_o, newm_o, *, k_sel):
    n, past = keys_ref.shape
    key_new = newkey_ref[:, :1]
    idx = lax.broadcasted_iota(I32, (n, past), 1)

    def total(ind_past, ind_new):
        return jnp.sum(ind_past.astype(I32), axis=1, keepdims=True) + ind_new.astype(I32)

    lo0 = jnp.full((n, 1), KEY_NEG_INF + 1, I32)
    hi0 = jnp.full((n, 1), KEY_MAX, I32)
    thr, c_thr, c_above = _bisect_kth(lambda mid: total(keys_ref[...] >= mid, key_new >= mid),
                                      lo0, hi0, jnp.full((n, 1), past + 1, I32), jnp.zeros((n, 1), I32), k_sel)
    has_tie = c_thr > k_sel
    need = k_sel - c_above
    tie_j = _bisect_tie_index(
        lambda mid: total(jnp.logical_and(keys_ref[...] == thr, idx <= mid),
                          jnp.logical_and(key_new == thr, past <= mid)),
        need, jnp.max(has_tie.astype(I32)), past, 14)
    tie_j = jnp.where(has_tie, tie_j, KEY_MAX)
    t = keys_ref[...]
    tie_ok = jnp.where(idx <= tie_j, 0.0, NEG_BIG)
    madd_o[...] = jnp.where(t > thr, 0.0, jnp.where(t == thr, tie_ok, NEG_BIG)).astype(F32)
    new_ok = jnp.where(past <= tie_j, 0.0, NEG_BIG)
    newm = jnp.where(key_new > thr, 0.0, jnp.where(key_new == thr, new_ok, NEG_BIG)).astype(F32)
    newm_o[...] = jnp.broadcast_to(newm, (n, PAGE_SIZE))


def _sample_select(keys, newkey):
    n, past = keys.shape
    k_sel = min(TOPK_MAX, (past + 1) // 4)
    return pl.pallas_call(
        functools.partial(_sample_select_kernel, k_sel=k_sel),
        grid=(1,),
        in_specs=[_const_spec(keys.shape), _const_spec(newkey.shape)],
        out_specs=[_const_spec(keys.shape), _const_spec(newkey.shape)],
        out_shape=[jax.ShapeDtypeStruct(keys.shape, F32), jax.ShapeDtypeStruct(newkey.shape, F32)],
        compiler_params=_params(("arbitrary",)),
        name="sample_select",
    )(keys, newkey)


def _attn_sample_kernel(pt_ref, q_ref, kn_ref, vn_ref, madd_ref, newm_ref, bias_ref, biasn_ref, *rest, n_pages):
    gp = SAMPLE_KV_PAGES
    k_pages = rest[:gp]
    v_pages = rest[gp:2 * gp]
    yb_o, m_scr, l_scr, acc_scr = rest[2 * gp:]
    c = pl.program_id(1)
    n_chunks = n_pages // gp
    head_of_lane = lax.broadcasted_iota(I32, (N_HEADS, ATT_WIDTH), 1) // HEAD_DIM
    head_of_row = lax.broadcasted_iota(I32, (N_HEADS, ATT_WIDTH), 0)
    own = head_of_lane == head_of_row
    q_blk = jnp.where(own, jnp.broadcast_to(q_ref[0], (N_HEADS, ATT_WIDTH)), 0.0)

    @pl.when(c == 0)
    def _():
        kn = kn_ref[0].astype(BF16).astype(F32)
        x_new = (jnp.sum(q_blk.astype(BF16).astype(F32) * kn, axis=1, keepdims=True)
                 + biasn_ref[...] + newm_ref[0][:, :1])
        m_scr[...] = x_new
        l_scr[...] = jnp.ones_like(l_scr)
        acc_scr[...] = jnp.broadcast_to(vn_ref[0].astype(BF16).astype(F32), (N_HEADS, ATT_WIDTH))

    q_bf = q_blk.astype(BF16)
    xs = []
    for g in range(gp):
        pg = c * gp + g
        kt = k_pages[g][0].astype(BF16)
        xs.append(_dot(q_bf, kt) + madd_ref[0, pg] + bias_ref[pg])
    x = jnp.concatenate(xs, axis=1)
    m = m_scr[...]
    m_new = jnp.maximum(m, jnp.max(x, axis=1, keepdims=True))
    alpha = jnp.exp2(m - m_new)
    p = jnp.exp2(x - m_new)
    l_scr[...] = alpha * l_scr[...] + jnp.sum(p, axis=1, keepdims=True)
    p_bf = p.astype(BF16)
    pv = jnp.zeros((N_HEADS, ATT_WIDTH), F32)
    for g in range(gp):
        vt = v_pages[g][0].astype(BF16)
        pv = pv + _dot_nt(p_bf[:, g * PAGE_SIZE:(g + 1) * PAGE_SIZE], vt)
    acc_scr[...] = alpha * acc_scr[...] + pv
    m_scr[...] = m_new

    @pl.when(c == n_chunks - 1)
    def _():
        o = acc_scr[...] / l_scr[...]
        yb_o[0] = jnp.sum(jnp.where(own, o, 0.0), axis=0, keepdims=True).astype(BF16)


def _attn_sample(page_table, q, k_new, v_new, madd, newm, bias_pages, bias_new, cache_kt, cache_vt):
    n, n_pages = page_table.shape
    gp = SAMPLE_KV_PAGES
    assert n_pages % gp == 0
    page_spec = lambda g: pl.BlockSpec((1, ATT_WIDTH, PAGE_SIZE), lambda b, c, pt: (pt[b, c * gp + g], 0, 0))
    per_seq = lambda shape: pl.BlockSpec((1,) + shape, lambda b, c, pt: (b, 0, 0))
    grid_spec = pltpu.PrefetchScalarGridSpec(
        num_scalar_prefetch=1,
        grid=(n, n_pages // gp),
        in_specs=[per_seq((1, ATT_WIDTH)), per_seq((1, ATT_WIDTH)), per_seq((1, ATT_WIDTH)),
                  pl.BlockSpec((1, n_pages, 1, PAGE_SIZE), lambda b, c, pt: (b, 0, 0, 0)), per_seq((1, PAGE_SIZE)),
                  pl.BlockSpec(bias_pages.shape, lambda b, c, pt: (0, 0, 0)),
                  pl.BlockSpec(bias_new.shape, lambda b, c, pt: (0, 0))]
        + [page_spec(g) for g in range(gp)] * 2,
        out_specs=per_seq((1, ATT_WIDTH)),
        scratch_shapes=[pltpu.VMEM((N_HEADS, 1), F32), pltpu.VMEM((N_HEADS, 1), F32),
                        pltpu.VMEM((N_HEADS, ATT_WIDTH), F32)],
    )
    return pl.pallas_call(
        functools.partial(_attn_sample_kernel, n_pages=n_pages),
        grid_spec=grid_spec,
        out_shape=jax.ShapeDtypeStruct((n, 1, ATT_WIDTH), BF16),
        compiler_params=_params(("parallel", "arbitrary")),
        name="attn_sample",
    )(page_table, q, k_new, v_new, madd, newm, bias_pages, bias_new, *([cache_kt] * gp), *([cache_vt] * gp))


def _merge_kernel(x_ref, ya_ref, yb_ref, sga_ref, sgb_ref, wa, wb, wo, g_ref, b_ref, x1_o, *, alpha):
    merged = (sga_ref[...].astype(F32) * _dot(ya_ref[...], wa[...])
              + sgb_ref[...].astype(F32) * _dot(yb_ref[...], wb[...]))
    z = alpha * x_ref[...] + _dot(merged.astype(BF16), wo[...])
    x1_o[...] = _layer_norm(z, g_ref[...], b_ref[...])


def _merge(x, ya, yb, sga, sgb, wa, wb, wo, g, b, alpha):
    n = x.shape[0]
    tm = min(TOKEN_TILE, n)
    rows = lambda c: pl.BlockSpec((tm, c), lambda i: (i, 0))
    return pl.pallas_call(
        functools.partial(_merge_kernel, alpha=alpha),
        grid=(n // tm,),
        in_specs=[rows(D_MODEL), rows(SSM_WIDTH), rows(ATT_WIDTH), rows(D_MODEL), rows(D_MODEL),
                  _const_spec(wa.shape), _const_spec(wb.shape), _const_spec(wo.shape),
                  _const_spec(g.shape), _const_spec(b.shape)],
        out_specs=rows(D_MODEL),
        out_shape=jax.ShapeDtypeStruct((n, D_MODEL), F32),
        compiler_params=_params(("parallel",)),
        name="merge_out_proj",
    )(x, ya, yb, sga, sgb, wa, wb, wo, g, b)


def _ffn_kernel(x1_ref, wg, wu, wd, g_ref, b_ref, y_o, acc_scr, *, alpha):
    x1 = x1_ref[...]
    xb = x1.astype(BF16)
    acc_scr[...] = alpha * x1
    for c in range(D_FF // FF_CHUNK):
        sl = slice(c * FF_CHUNK, (c + 1) * FF_CHUNK)
        gate = _dot(xb, wg[:, sl])
        up = _dot(xb, wu[:, sl])
        hid = (jax.nn.silu(gate) * up).astype(BF16)
        acc_scr[...] += _dot(hid, wd[sl, :])
    y_o[...] = _layer_norm(acc_scr[...], g_ref[...], b_ref[...])


def _ffn(x1, wg, wu, wd, g, b, alpha):
    n = x1.shape[0]
    tm = min(TOKEN_TILE, n)
    rows = pl.BlockSpec((tm, D_MODEL), lambda i: (i, 0))
    return pl.pallas_call(
        functools.partial(_ffn_kernel, alpha=alpha),
        grid=(n // tm,),
        in_specs=[rows, _const_spec(wg.shape), _const_spec(wu.shape), _const_spec(wd.shape),
                  _const_spec(g.shape), _const_spec(b.shape)],
        out_specs=rows,
        out_shape=jax.ShapeDtypeStruct((n, D_MODEL), F32),
        scratch_shapes=[pltpu.VMEM((tm, D_MODEL), F32)],
        compiler_params=_params(("parallel",)),
        name="ffn",
    )(x1, wg, wu, wd, g, b)


def _rel_bucket(rel):
    n = jnp.maximum(rel, 0)
    max_exact = REL_BUCKETS // 2
    nf = jnp.maximum(n, max_exact).astype(F32)
    large = max_exact + (jnp.log(nf / max_exact) / math.log(REL_MAX_DIST / max_exact)
                         * (REL_BUCKETS - max_exact)).astype(I32)
    large = jnp.minimum(large, REL_BUCKETS - 1)
    return jnp.where(n < max_exact, n, large)


def _bias_lookup(rel_bias, rel):
    onehot = (_rel_bucket(rel)[..., None] == jnp.arange(REL_BUCKETS, dtype=I32)).astype(F32)
    table = rel_bias.astype(F32) * LOG2E
    return jnp.einsum('...k,kh->...h', onehot, table, precision=lax.Precision.HIGHEST)


def _prompt_bias_tiles(rel_bias):
    kl = jnp.arange(KV_TILE, dtype=I32)[:, None]
    ql = jnp.arange(Q_TILE, dtype=I32)[None, :]
    rel = jnp.stack([ql - kl, ql - kl + KV_TILE])
    assert KV_TILE + 1 >= REL_MAX_DIST
    near = jnp.moveaxis(_bias_lookup(rel_bias, rel), -1, 0)
    far = _bias_lookup(rel_bias, jnp.full((), 2 * KV_TILE, I32))
    return near, far


def _ssm_operators(a_re, a_im, log_dt, b_re, b_im, c_re, c_im, d):
    g, p, gi = SSM_GROUPS, SSM_STATE, SSM_GROUP
    a_re, a_im = a_re.astype(F32), a_im.astype(F32)
    dt = jnp.exp(log_dt.astype(F32))[:, None]
    mag = jnp.exp(dt * a_re)
    lre = mag * jnp.cos(dt * a_im)
    lim = mag * jnp.sin(dt * a_im)
    den = a_re * a_re + a_im * a_im
    fr = ((lre - 1.0) * a_re + lim * a_im) / den
    fi = (lim * a_re - (lre - 1.0) * a_im) / den
    bbr = fr[:, :, None] * b_re - fi[:, :, None] * b_im
    bbi = fr[:, :, None] * b_im + fi[:, :, None] * b_re
    eye = jnp.eye(g, dtype=F32)
    blockdiag_in = lambda m: jnp.einsum('gpi,gh->gihp', m, eye).reshape(g * gi, g * p)
    blockdiag_out = lambda m: jnp.einsum('gip,gh->gphi', m, eye).reshape(g * p, g * gi)
    bfull = jnp.concatenate([blockdiag_in(bbr), blockdiag_in(bbi)], axis=1)
    cfull = jnp.concatenate([blockdiag_out(c_re.astype(F32)), blockdiag_out(-c_im.astype(F32))], axis=0)
    return {'lre': lre.reshape(1, g * p), 'lim': lim.reshape(1, g * p),
            'bfull': bfull.astype(BF16), 'cfull': cfull.astype(BF16), 'd': d.astype(F32).reshape(1, g * gi)}


def _split_w_in(w_in):
    offs = [0]
    for n in IN_SPLITS:
        offs.append(offs[-1] + n)
    names = ['u', 'q', 'k', 'v', 'qi', 'ki', 'wi', 'ga', 'gb']
    return {nm: w_in[:, offs[j]:offs[j + 1]] for j, nm in enumerate(names)}


def kernel(x_prompt, x_sample, cache_k, cache_v, cache_idx_k, state_ssm_re, state_ssm_im, page_table, w_in, ssm_a_re, ssm_a_im, ssm_log_dt, ssm_b_re, ssm_b_im, ssm_c_re, ssm_c_im, ssm_d, w_glu, w_branch_a, w_branch_b, rel_bias, w_out, ln1_g, ln1_b, w_ffn_in, w_ffn_out, ln2_g, ln2_b):
    depth = w_in.shape[0]
    bsz, s_len, _ = x_prompt.shape
    n_dec, dec_seq, _ = x_sample.shape
    assert dec_seq == 1
    n_pages = page_table.shape[1]
    n_pool = cache_k.shape[1]
    past = n_pages * PAGE_SIZE
    alpha = (2 * depth) ** 0.25
    idx_scale = IDX_DIM ** -0.5 * IDX_HEADS ** -0.5
    att_scale = HEAD_DIM ** -0.5 * LOG2E

    bias_t, bias_far = _prompt_bias_tiles(rel_bias)
    key_pos = jnp.arange(past, dtype=I32).reshape(n_pages, PAGE_SIZE)
    bias_pages = jnp.swapaxes(_bias_lookup(rel_bias, past - key_pos), 1, 2)
    bias_new = _bias_lookup(rel_bias, jnp.zeros((), I32)).reshape(N_HEADS, 1)

    x_p = x_prompt
    x_s = x_sample.reshape(n_dec, D_MODEL)
    outs = {k: [] for k in ('kp', 'vp', 'kip', 'hpr', 'hpi', 'ks', 'vs', 'kis', 'hsr', 'hsi')}
    for l in range(depth):
        w = _split_w_in(w_in[l])
        wn = {nm: w[nm].astype(BF16) for nm in ('u', 'k', 'v', 'ki', 'ga', 'gb')}
        wn['tq'] = (w['q'] * att_scale).T.astype(BF16)
        wn['tv'] = w['v'].T.astype(BF16)
        wn['tqi'] = w['qi'].T.astype(BF16)
        wn['twi'] = jnp.pad((w['wi'] * idx_scale).T, ((0, 16 - IDX_HEADS), (0, 0))).astype(BF16)
        ws = {nm: w[nm].astype(BF16) for nm in ('u', 'k', 'v', 'qi', 'ki', 'ga', 'gb')}
        ws['q'] = (w['q'] * att_scale).astype(BF16)
        ws['wi'] = (w['wi'] * idx_scale).astype(BF16)
        ops = _ssm_operators(ssm_a_re[l], ssm_a_im[l], ssm_log_dt[l], ssm_b_re[l], ssm_b_im[l],
                             ssm_c_re[l], ssm_c_im[l], ssm_d[l])
        wglu = w_glu[l].astype(BF16)
        wa, wb, wo = w_branch_a[l].astype(BF16), w_branch_b[l].astype(BF16), w_out[l].astype(BF16)
        wg, wu = w_ffn_in[l][:, :D_FF].astype(BF16), w_ffn_in[l][:, D_FF:].astype(BF16)
        wd = w_ffn_out[l].astype(BF16)
        g1, b1 = ln1_g[l].astype(F32).reshape(1, D_MODEL), ln1_b[l].astype(F32).reshape(1, D_MODEL)
        g2, b2 = ln2_g[l].astype(F32).reshape(1, D_MODEL), ln2_b[l].astype(F32).reshape(1, D_MODEL)

        u, k, v, ki, k_bf, ki_bf, sga, sgb, qt, vt, qit, wit = _in_proj_prompt(x_p, wn)
        ya, hpr, hpi = _ssm_prompt(u, ops, wglu)
        yb = _attn_prompt(qt, qit, wit, k_bf, vt, ki_bf, bias_t, bias_far)
        n_tok = bsz * s_len
        flat = lambda a: a.reshape(n_tok, a.shape[-1])
        x1 = _merge(flat(x_p), flat(ya), flat(yb), flat(sga), flat(sgb), wa, wb, wo, g1, b1, alpha)
        x_p = _ffn(x1, wg, wu, wd, g2, b2, alpha).reshape(bsz, s_len, D_MODEL)
        outs['kp'].append(k.reshape(bsz, s_len, N_HEADS, HEAD_DIM))
        outs['vp'].append(v.reshape(bsz, s_len, N_HEADS, HEAD_DIM))
        outs['kip'].append(ki)
        outs['hpr'].append(hpr.reshape(bsz, SSM_GROUPS, SSM_STATE))
        outs['hpi'].append(hpi.reshape(bsz, SSM_GROUPS, SSM_STATE))

        us, qs, k_s, v_s, qis, kis, wis, sga_s, sgb_s = _in_proj_sample(x_s, ws)
        ya_s, hsr, hsi = _ssm_sample(us, state_ssm_re[l].reshape(n_dec, SSM_COLS).astype(F32),
                                     state_ssm_im[l].reshape(n_dec, SSM_COLS).astype(F32), ops, wglu)
        cache_kt = jnp.transpose(cache_k[l], (0, 2, 3, 1)).reshape(n_pool, ATT_WIDTH, PAGE_SIZE)
        cache_vt = jnp.transpose(cache_v[l], (0, 2, 3, 1)).reshape(n_pool, ATT_WIDTH, PAGE_SIZE)
        cache_it = jnp.transpose(cache_idx_k[l], (0, 2, 1))
        keys, newkey = _sample_scores(
            page_table, qis.astype(BF16).reshape(n_dec, IDX_HEADS, IDX_DIM), wis.reshape(n_dec, IDX_HEADS, 1),
            kis.astype(BF16).reshape(n_dec, 1, IDX_DIM), cache_it)
        madd, newm = _sample_select(keys.reshape(n_dec, past), newkey.reshape(n_dec, PAGE_SIZE))
        yb_s = _attn_sample(page_table, qs.reshape(n_dec, 1, ATT_WIDTH), k_s.reshape(n_dec, 1, ATT_WIDTH),
                            v_s.reshape(n_dec, 1, ATT_WIDTH), madd.reshape(n_dec, n_pages, 1, PAGE_SIZE),
                            newm.reshape(n_dec, 1, PAGE_SIZE), bias_pages, bias_new, cache_kt, cache_vt)
        x1_s = _merge(x_s, ya_s, yb_s.reshape(n_dec, ATT_WIDTH), sga_s, sgb_s, wa, wb, wo, g1, b1, alpha)
        x_s = _ffn(x1_s, wg, wu, wd, g2, b2, alpha)
        outs['ks'].append(k_s.reshape(n_dec, 1, N_HEADS, HEAD_DIM))
        outs['vs'].append(v_s.reshape(n_dec, 1, N_HEADS, HEAD_DIM))
        outs['kis'].append(kis.reshape(n_dec, 1, IDX_DIM))
        outs['hsr'].append(hsr.reshape(n_dec, SSM_GROUPS, SSM_STATE))
        outs['hsi'].append(hsi.reshape(n_dec, SSM_GROUPS, SSM_STATE))

    st = lambda name: jnp.stack(outs[name])
    return (x_p, x_s.reshape(n_dec, 1, D_MODEL),
            st('kp'), st('vp'), st('kip'), st('hpr'), st('hpi'),
            st('ks'), st('vs'), st('kis'), st('hsr'), st('hsi'))
```

```python
import functools
import math

import jax
import jax.numpy as jnp
from jax import lax
from jax.experimental import pallas as pl
from jax.experimental.pallas import tpu as pltpu

F32 = jnp.float32
BF16 = jnp.bfloat16
I32 = jnp.int32

D_MODEL = 1024
SSM_WIDTH = 512
SSM_GROUP = 16
SSM_GROUPS = 32
SSM_STATE = 64
SSM_COLS = SSM_GROUPS * SSM_STATE
N_HEADS = 8
HEAD_DIM = 64
ATT_WIDTH = 512
IDX_HEADS = 8
IDX_DIM = 32
TOPK_MAX = 256
PAGE_SIZE = 128
REL_BUCKETS = 32
REL_MAX_DIST = 128
D_FF = 2816
LN_EPS = 1e-5
IN_SPLITS = (SSM_WIDTH, ATT_WIDTH, ATT_WIDTH, ATT_WIDTH, IDX_HEADS * IDX_DIM, IDX_DIM, IDX_HEADS, D_MODEL, D_MODEL)
LOG2E = math.log2(math.e)

VMEM_LIMIT_BYTES = 56 * 1024 * 1024
NEG_BIG = -1e30
KEY_NEG_INF = -2139095041
KEY_MAX = 2147483647

TOKEN_TILE = 512
Q_TILE = 256
KV_TILE = 256
SSM_T_TILE = 64
FF_CHUNK = 256
SAMPLE_IDX_PAGES = 32
SAMPLE_KV_PAGES = 16


def _params(sem):
    return pltpu.CompilerParams(dimension_semantics=sem, vmem_limit_bytes=VMEM_LIMIT_BYTES)


def _const_spec(shape):
    n = len(shape)
    return pl.BlockSpec(shape, lambda *_: (0,) * n)


def _resident_spec(block, index_map):
    return pl.BlockSpec(block, index_map, pipeline_mode=pl.Buffered(1))


def _dot(a, b):
    return jnp.dot(a, b, preferred_element_type=F32)


def _dot_nt(a, b):
    return lax.dot_general(a, b, (((1,), (1,)), ((), ())), preferred_element_type=F32)


def _layer_norm(z, g, b):
    mu = jnp.mean(z, axis=-1, keepdims=True)
    zc = z - mu
    var = jnp.mean(zc * zc, axis=-1, keepdims=True)
    return zc * lax.rsqrt(var + LN_EPS) * g + b


def _ordered_key(x):
    b = pltpu.bitcast(x, I32)
    return b ^ ((b >> 31) & KEY_MAX)


def _in_proj_prompt_kernel(x_ref, wu, wk, wv, wga, wgb, wki, wtq, wtv, wtqi, wtwi,
                           u_o, k_o, v_o, ki_o, kbf_o, kibf_o, sga_o, sgb_o, qt_o, vt_o, qit_o, wit_o):
    xb = x_ref[0].astype(BF16)
    u_o[0] = _dot(xb, wu[...])
    k = _dot(xb, wk[...])
    k_o[0] = k
    kbf_o[0] = k.astype(BF16)
    v_o[0] = _dot(xb, wv[...])
    ki = _dot(xb, wki[...])
    ki_o[0] = ki
    kibf_o[0] = ki.astype(BF16)
    sga_o[0] = jax.nn.sigmoid(_dot(xb, wga[...])).astype(BF16)
    sgb_o[0] = jax.nn.sigmoid(_dot(xb, wgb[...])).astype(BF16)
    qt_o[0] = _dot_nt(wtq[...], xb).astype(BF16)
    vt_o[0] = _dot_nt(wtv[...], xb).astype(BF16)
    qit_o[0] = _dot_nt(wtqi[...], xb).astype(BF16)
    wit_o[0] = _dot_nt(wtwi[...], xb)[:IDX_HEADS]


def _in_proj_prompt(x, w):
    bsz, s_len, _ = x.shape
    tm = min(TOKEN_TILE, s_len)
    nat = lambda c: pl.BlockSpec((1, tm, c), lambda b, i: (b, i, 0))
    tr = lambda c: pl.BlockSpec((1, c, tm), lambda b, i: (b, 0, i))
    sds = jax.ShapeDtypeStruct
    weights = [w['u'], w['k'], w['v'], w['ga'], w['gb'], w['ki'], w['tq'], w['tv'], w['tqi'], w['twi']]
    return pl.pallas_call(
        _in_proj_prompt_kernel,
        grid=(bsz, s_len // tm),
        in_specs=[nat(D_MODEL)] + [_const_spec(a.shape) for a in weights],
        out_specs=[nat(SSM_WIDTH), nat(ATT_WIDTH), nat(ATT_WIDTH), nat(IDX_DIM), nat(ATT_WIDTH), nat(IDX_DIM),
                   nat(D_MODEL), nat(D_MODEL), tr(ATT_WIDTH), tr(ATT_WIDTH), tr(IDX_HEADS * IDX_DIM), tr(IDX_HEADS)],
        out_shape=[sds((bsz, s_len, SSM_WIDTH), F32), sds((bsz, s_len, ATT_WIDTH), F32),
                   sds((bsz, s_len, ATT_WIDTH), F32), sds((bsz, s_len, IDX_DIM), F32),
                   sds((bsz, s_len, ATT_WIDTH), BF16), sds((bsz, s_len, IDX_DIM), BF16),
                   sds((bsz, s_len, D_MODEL), BF16), sds((bsz, s_len, D_MODEL), BF16),
                   sds((bsz, ATT_WIDTH, s_len), BF16), sds((bsz, ATT_WIDTH, s_len), BF16),
                   sds((bsz, IDX_HEADS * IDX_DIM, s_len), BF16), sds((bsz, IDX_HEADS, s_len), F32)],
        compiler_params=_params(("parallel", "parallel")),
        name="in_proj_prompt",
    )(x, *weights)


def _in_proj_sample_kernel(x_ref, wu, wq, wk, wv, wqi, wki, wwi, wga, wgb,
                           u_o, q_o, k_o, v_o, qi_o, ki_o, wi_o, sga_o, sgb_o):
    xb = x_ref[...].astype(BF16)
    u_o[...] = _dot(xb, wu[...])
    q_o[...] = _dot(xb, wq[...])
    k_o[...] = _dot(xb, wk[...])
    v_o[...] = _dot(xb, wv[...])
    qi_o[...] = _dot(xb, wqi[...])
    ki_o[...] = _dot(xb, wki[...])
    wi_o[...] = _dot(xb, wwi[...])
    sga_o[...] = jax.nn.sigmoid(_dot(xb, wga[...])).astype(BF16)
    sgb_o[...] = jax.nn.sigmoid(_dot(xb, wgb[...])).astype(BF16)


def _in_proj_sample(x, w):
    n = x.shape[0]
    weights = [w['u'], w['q'], w['k'], w['v'], w['qi'], w['ki'], w['wi'], w['ga'], w['gb']]
    cols = [SSM_WIDTH, ATT_WIDTH, ATT_WIDTH, ATT_WIDTH, IDX_HEADS * IDX_DIM, IDX_DIM, IDX_HEADS, D_MODEL, D_MODEL]
    dts = [F32] * 7 + [BF16, BF16]
    return pl.pallas_call(
        _in_proj_sample_kernel,
        grid=(1,),
        in_specs=[_const_spec(x.shape)] + [_const_spec(a.shape) for a in weights],
        out_specs=[_const_spec((n, c)) for c in cols],
        out_shape=[jax.ShapeDtypeStruct((n, c), d) for c, d in zip(cols, dts)],
        compiler_params=_params(("arbitrary",)),
        name="in_proj_sample",
    )(x, *weights)


def _glu_gate(y, wglu):
    y = jax.nn.gelu(y)
    return y * jax.nn.sigmoid(_dot(y.astype(BF16), wglu[...]))


def _glu_tail(h_bf, u, cfull, dvec, wglu):
    return _glu_gate(_dot(h_bf, cfull[...]) + dvec[...] * u, wglu)


SSM_SLABS = SSM_WIDTH // 128
SLAB_COLS = SSM_COLS // SSM_SLABS


def _ssm_prompt_kernel(u_ref, perm_ref, permt_ref, bslab, cslab, lre_ref, lim_ref, dvec, wglu,
                       ya_o, hre_o, him_o, h_scr, u_scr, y_scr, state_scr, *, bsz, tt):
    c = pl.program_id(0)
    rows = bsz * tt
    row_chunk = 256

    @pl.when(c == 0)
    def _():
        state_scr[...] = jnp.zeros_like(state_scr)

    u_bt = jnp.concatenate([u_ref[b] for b in range(bsz)], axis=0)
    u_hi = u_bt.astype(BF16)
    u_lo = (u_bt - u_hi.astype(F32)).astype(BF16)
    u_scr[...] = _dot(perm_ref[...], u_hi) + _dot(perm_ref[...], u_lo)
    for r in range(rows // row_chunk):
        sl = slice(r * row_chunk, (r + 1) * row_chunk)
        u_bf = u_scr[sl, :].astype(BF16)
        for s in range(SSM_SLABS):
            bu = _dot(u_bf[:, s * 128:(s + 1) * 128], bslab[s])
            h_scr[sl, s * SLAB_COLS:(s + 1) * SLAB_COLS] = bu[:, :SLAB_COLS]
            h_scr[sl, SSM_COLS + s * SLAB_COLS:SSM_COLS + (s + 1) * SLAB_COLS] = bu[:, SLAB_COLS:]

    chunk = 512
    for ci in range(SSM_COLS // chunk):
        re_sl = slice(ci * chunk, (ci + 1) * chunk)
        im_sl = slice(SSM_COLS + ci * chunk, SSM_COLS + (ci + 1) * chunk)
        lre = lre_ref[:, re_sl]
        lim = lim_ref[:, re_sl]

        def step(t, carry):
            hr, hi = carry
            r0 = pl.multiple_of(t * bsz, bsz)
            nr = lre * hr - lim * hi + h_scr[pl.ds(r0, bsz), re_sl]
            ni = lre * hi + lim * hr + h_scr[pl.ds(r0, bsz), im_sl]
            h_scr[pl.ds(r0, bsz), re_sl] = nr
            h_scr[pl.ds(r0, bsz), im_sl] = ni
            return nr, ni

        hr, hi = lax.fori_loop(0, tt, step, (state_scr[:, re_sl], state_scr[:, im_sl]), unroll=4)
        state_scr[:, re_sl] = hr
        state_scr[:, im_sl] = hi

    for r in range(rows // row_chunk):
        sl = slice(r * row_chunk, (r + 1) * row_chunk)
        ys = []
        for s in range(SSM_SLABS):
            h_re = h_scr[sl, s * SLAB_COLS:(s + 1) * SLAB_COLS].astype(BF16)
            h_im = h_scr[sl, SSM_COLS + s * SLAB_COLS:SSM_COLS + (s + 1) * SLAB_COLS].astype(BF16)
            ys.append(_dot(h_re, cslab[s, :SLAB_COLS]) + _dot(h_im, cslab[s, SLAB_COLS:]))
        y = jnp.concatenate(ys, axis=1) + dvec[...] * u_scr[sl, :]
        y_scr[sl, :] = _glu_gate(y, wglu).astype(BF16)

    ya_bt = _dot(permt_ref[...], y_scr[...]).astype(BF16)
    for b in range(bsz):
        ya_o[b] = ya_bt[b * tt:(b + 1) * tt]

    @pl.when(c == pl.num_programs(0) - 1)
    def _():
        hre_o[...] = state_scr[:, :SSM_COLS]
        him_o[...] = state_scr[:, SSM_COLS:]


def _ssm_prompt(u, ops, wglu):
    bsz, s_len, _ = u.shape
    tt = min(SSM_T_TILE, s_len)
    rows = bsz * tt
    lre = jnp.broadcast_to(ops['lre'], (bsz, SSM_COLS))
    lim = jnp.broadcast_to(ops['lim'], (bsz, SSM_COLS))
    kern = functools.partial(_ssm_prompt_kernel, bsz=bsz, tt=tt)
    src = (jnp.arange(rows, dtype=I32) % bsz) * tt + jnp.arange(rows, dtype=I32) // bsz
    perm = (src[:, None] == jnp.arange(rows, dtype=I32)[None, :]).astype(BF16)
    return pl.pallas_call(
        kern,
        grid=(s_len // tt,),
        in_specs=[pl.BlockSpec((bsz, tt, SSM_WIDTH), lambda c: (0, c, 0)),
                  _const_spec(perm.shape), _const_spec(perm.shape),
                  _const_spec(ops['bslab'].shape), _const_spec(ops['cslab'].shape),
                  _const_spec(lre.shape), _const_spec(lim.shape), _const_spec(ops['d'].shape),
                  _const_spec(wglu.shape)],
        out_specs=[pl.BlockSpec((bsz, tt, SSM_WIDTH), lambda c: (0, c, 0)),
                   _const_spec((bsz, SSM_COLS)), _const_spec((bsz, SSM_COLS))],
        out_shape=[jax.ShapeDtypeStruct((bsz, s_len, SSM_WIDTH), BF16),
                   jax.ShapeDtypeStruct((bsz, SSM_COLS), F32), jax.ShapeDtypeStruct((bsz, SSM_COLS), F32)],
        scratch_shapes=[pltpu.VMEM((rows, 2 * SSM_COLS), F32), pltpu.VMEM((rows, SSM_WIDTH), F32),
                        pltpu.VMEM((rows, SSM_WIDTH), BF16), pltpu.VMEM((bsz, 2 * SSM_COLS), F32)],
        compiler_params=_params(("arbitrary",)),
        name="ssm_prompt",
    )(u, perm, perm.T, ops['bslab'], ops['cslab'], lre, lim, ops['d'], wglu)


def _ssm_sample_kernel(u_ref, h0re_ref, h0im_ref, bfull, cfull, lre_ref, lim_ref, dvec, wglu,
                       ya_o, hre_o, him_o):
    u = u_ref[...]
    bu = _dot(u.astype(BF16), bfull[...])
    lre = lre_ref[...]
    lim = lim_ref[...]
    h0r = h0re_ref[...]
    h0i = h0im_ref[...]
    hr = lre * h0r - lim * h0i + bu[:, :SSM_COLS]
    hi = lre * h0i + lim * h0r + bu[:, SSM_COLS:]
    hre_o[...] = hr
    him_o[...] = hi
    h_bf = jnp.concatenate([hr.astype(BF16), hi.astype(BF16)], axis=1)
    ya_o[...] = _glu_tail(h_bf, u, cfull, dvec, wglu).astype(BF16)


def _ssm_sample(u, h0re, h0im, ops, wglu):
    n = u.shape[0]
    args = [u, h0re, h0im, ops['bfull'], ops['cfull'], ops['lre'], ops['lim'], ops['d'], wglu]
    return pl.pallas_call(
        _ssm_sample_kernel,
        grid=(1,),
        in_specs=[_const_spec(a.shape) for a in args],
        out_specs=[_const_spec((n, SSM_WIDTH)), _const_spec((n, SSM_COLS)), _const_spec((n, SSM_COLS))],
        out_shape=[jax.ShapeDtypeStruct((n, SSM_WIDTH), BF16),
                   jax.ShapeDtypeStruct((n, SSM_COLS), F32), jax.ShapeDtypeStruct((n, SSM_COLS), F32)],
        compiler_params=_params(("arbitrary",)),
        name="ssm_sample",
    )(*args)


def _bisect_step(count_ge, k, st):
    lo, hi, c_lo, c_hi = st
    mid = (lo >> 1) + (hi >> 1) + (lo & hi & 1)
    cnt = count_ge(mid)
    active = lo + 1 != hi
    up = jnp.logical_and(active, cnt >= k)
    down = jnp.logical_and(active, cnt < k)
    n_lo = jnp.where(up, mid, lo)
    n_clo = jnp.where(up, cnt, c_lo)
    n_hi = jnp.where(down, mid, jnp.where(jnp.logical_and(active, cnt == k), mid + 1, hi))
    n_chi = jnp.where(down, cnt, c_hi)
    return n_lo, n_hi, n_clo, n_chi


def _bisect_kth(count_ge, lo, hi, c_lo, c_hi, k, steps_per_check=1):
    def cond(st):
        it, lo, hi, _, _ = st
        return jnp.logical_and(it < 34, jnp.max((lo + 1 != hi).astype(I32)) > 0)

    def body(st):
        it, st = st[0], st[1:]
        for _ in range(steps_per_check):
            st = _bisect_step(count_ge, k, st)
        return (it + steps_per_check,) + st

    _, lo, hi, c_lo, c_hi = lax.while_loop(cond, body, (jnp.int32(0), lo, hi, c_lo, c_hi))
    return lo, c_lo, c_hi


def _bisect_kth_fixed(count_ge, lo, hi, c_lo, c_hi, k):
    width = hi - lo
    n_steps = jnp.max(32 - lax.clz(width - 1))
    lo, hi, c_lo, c_hi = lax.fori_loop(0, n_steps, lambda _, st: _bisect_step(count_ge, k, st),
                                       (lo, hi, c_lo, c_hi))
    return lo, c_lo, c_hi


def _bisect_tie_index(count_tie_le, need, n_tie_rows, idx_hi, n_bits):
    def cond(st):
        it, _, _ = st
        return jnp.logical_and(it < n_bits, n_tie_rows > 0)

    def body(st):
        it, lo, hi = st
        mid = (lo + hi) >> 1
        ok = count_tie_le(mid) >= need
        return it + 1, jnp.where(ok, lo, mid), jnp.where(ok, mid, hi)

    lo0 = jnp.full_like(need, -1)
    hi0 = jnp.full_like(need, idx_hi)
    _, _, hi = lax.while_loop(cond, body, (jnp.int32(0), lo0, hi0))
    return hi


def _attn_prompt_kernel(qt_ref, qit_ref, wit_ref, k_ref, vt_ref, ki_ref, bias_ref, bfar_ref, yb_o,
                        s_scr, s16_scr, o_scr, qm_scr, m_scr, l_scr, a_scr, off_scr, x_scr, *, k_sel):
    i = pl.program_id(1)
    n_tiles = i + 1
    tq = Q_TILE
    tk = KV_TILE
    q_pos = i * tq + lax.broadcasted_iota(I32, (1, tq), 1)
    kv_iota = lax.broadcasted_iota(I32, (tk, tq), 0)

    n_pairs = (n_tiles + 1) // 2

    def score_pair(jj, top):
        for half in range(2):
            r0 = pl.multiple_of((2 * jj + half) * tk, tk)
            ki_t = ki_ref[0, pl.ds(r0, tk), :]
            acc = jnp.zeros((tk, tq), F32)
            for h in range(IDX_HEADS):
                s = _dot(ki_t, qit_ref[0, h * IDX_DIM:(h + 1) * IDX_DIM, :])
                acc = acc + wit_ref[0, h:h + 1, :] * jnp.maximum(s, 0.0)
            acc = jnp.where(kv_iota + r0 <= q_pos, acc, -jnp.inf)
            key = _ordered_key(acc)
            s_scr[pl.ds(r0, tk), :] = key
            s16_scr[pl.ds(r0, tk), :] = (key >> 16).astype(jnp.int16)
            top = jnp.maximum(top, jnp.max(acc, axis=0, keepdims=True))
        return top

    top = lax.fori_loop(0, n_pairs, score_pair, jnp.full((1, tq), -jnp.inf, F32))

    def count_tiles(pred):
        def tile(jj, acc):
            for half in range(2):
                r0 = pl.multiple_of((2 * jj + half) * tk, tk)
                ind = pred(s_scr[pl.ds(r0, tk), :], r0).astype(I32)
                for r in range(tk // 8):
                    acc = acc + ind[r * 8:(r + 1) * 8]
            return acc
        acc = lax.fori_loop(0, n_pairs, tile, jnp.zeros((8, tq), I32))
        return jnp.sum(acc, axis=0, keepdims=True)

    def count_ge_high(mid):
        mid16 = mid.astype(jnp.int16)

        def tile(jj, acc):
            r0 = pl.multiple_of(jj * 2 * tk, 2 * tk)
            ind = jnp.where(s16_scr[pl.ds(r0, 2 * tk), :] >= mid16, jnp.int16(1), jnp.int16(0))
            for r in range(2 * tk // 16):
                acc = acc + ind[r * 16:(r + 1) * 16]
            return acc
        acc = lax.fori_loop(0, n_pairs, tile, jnp.zeros((16, tq), jnp.int16))
        return jnp.sum(acc.astype(I32), axis=0, keepdims=True)

    count_ge = lambda mid: count_tiles(lambda t, r0: t >= mid)
    n_valid = q_pos + 1
    searching = n_valid > k_sel
    zeros = jnp.zeros((1, tq), I32)
    floor_high = jnp.full((1, tq), (KEY_NEG_INF + 1) >> 16, I32)
    top_high = _ordered_key(top) >> 16
    guess = jnp.maximum(top_high - (4 << 7), 0)
    c_guess = count_ge_high(guess)
    use_guess = jnp.logical_and(jnp.logical_and(searching, top_high > 0), c_guess >= k_sel)
    lo_high = jnp.where(use_guess, guess, floor_high)
    c_lo_high = jnp.where(use_guess, c_guess, n_valid)
    hi_high = jnp.where(searching, top_high + 1, floor_high + 1)
    bucket, c_bucket, c_next = _bisect_kth_fixed(count_ge_high, lo_high, hi_high, c_lo_high, zeros, k_sel)
    settled = jnp.logical_or(jnp.logical_not(searching), c_bucket == k_sel)
    lo0 = bucket << 16
    bucket_end = jnp.where(bucket >= 32767, KEY_MAX, (bucket + 1) << 16)
    hi0 = jnp.where(settled, lo0 + 1, bucket_end)
    thr, c_thr, c_above = _bisect_kth(count_ge, lo0, hi0, c_bucket, c_next, k_sel, steps_per_check=2)
    has_tie = jnp.logical_and(searching, c_thr > k_sel)
    need = k_sel - c_above
    tie_j = _bisect_tie_index(
        lambda mid: count_tiles(lambda t, r0: jnp.logical_and(t == thr, kv_iota + r0 <= mid)),
        need, jnp.max(has_tie.astype(I32)), n_tiles * tk - 1, 14)
    tie_j = jnp.where(has_tie, tie_j, KEY_MAX)

    def mask_pair(jj, _):
        for half in range(2):
            r0 = pl.multiple_of((2 * jj + half) * tk, tk)
            t = s_scr[pl.ds(r0, tk), :]
            tie_ok = jnp.where(kv_iota + r0 <= tie_j, 0.0, NEG_BIG)
            madd = jnp.where(t > thr, 0.0, jnp.where(t == thr, tie_ok, NEG_BIG))
            s_scr[pl.ds(r0, tk), :] = pltpu.bitcast(madd.astype(F32), I32)
        return 0

    lax.fori_loop(0, n_pairs, mask_pair, 0)

    row = lax.broadcasted_iota(I32, (2 * HEAD_DIM, tq), 0)
    for pair in range(N_HEADS // 2):
        q_pair = qt_ref[0, pair * 128:(pair + 1) * 128, :]
        zero = jnp.zeros_like(q_pair)
        qm_scr[2 * pair] = jnp.where(row < HEAD_DIM, q_pair, zero)
        qm_scr[2 * pair + 1] = jnp.where(row >= HEAD_DIM, q_pair, zero)
    m_scr[...] = jnp.full(m_scr.shape, NEG_BIG, F32)
    l_scr[...] = jnp.zeros(l_scr.shape, F32)
    o_scr[...] = jnp.zeros(o_scr.shape, F32)

    def attn_tiles(tiles):
        for slot, (j, near_kind) in enumerate(tiles):
            r0 = pl.multiple_of(j * tk, tk)
            for pair in range(N_HEADS // 2):
                k_t = k_ref[0, pl.ds(r0, tk), pair * 128:(pair + 1) * 128]
                for h in (2 * pair, 2 * pair + 1):
                    x = _dot(k_t, qm_scr[h]) + pltpu.bitcast(s_scr[pl.ds(r0, tk), :], F32)
                    if near_kind is None:
                        c = bfar_ref[h]
                    else:
                        x = x + bias_ref[h, near_kind]
                        c = 0.0
                    x_scr[slot, h] = x
                    m = m_scr[h]
                    m_new = jnp.maximum(m, jnp.max(x, axis=0, keepdims=True) + c)
                    a_scr[slot, h] = jnp.exp2(m - m_new)
                    off_scr[slot, h] = m_new - c
                    m_scr[h] = m_new
        for slot, (j, _) in enumerate(tiles):
            r0 = pl.multiple_of(j * tk, tk)
            for h in range(N_HEADS):
                alpha = a_scr[slot, h]
                p = jnp.exp2(x_scr[slot, h] - off_scr[slot, h])
                l_scr[h] = alpha * l_scr[h] + jnp.sum(p, axis=0, keepdims=True)
                v_t = vt_ref[0, h * HEAD_DIM:(h + 1) * HEAD_DIM, pl.ds(r0, tk)]
                hs = slice(h * HEAD_DIM, (h + 1) * HEAD_DIM)
                o_scr[hs, :] = alpha * o_scr[hs, :] + _dot(v_t, p.astype(BF16))

    n_far = jnp.maximum(i - 1, 0)

    def far_pair(jj, _):
        attn_tiles([(2 * jj, None), (2 * jj + 1, None)])
        return 0

    lax.fori_loop(0, n_far // 2, far_pair, 0)

    @pl.when(n_far % 2 == 1)
    def _():
        attn_tiles([(n_far - 1, None)])

    @pl.when(i >= 1)
    def _():
        attn_tiles([(i - 1, 1), (i, 0)])

    @pl.when(i == 0)
    def _():
        attn_tiles([(i, 0)])

    for h in range(N_HEADS):
        hs = slice(h * HEAD_DIM, (h + 1) * HEAD_DIM)
        o_scr[hs, :] = o_scr[hs, :] / l_scr[h]
    yb_o[0] = o_scr[...].T.astype(BF16)


def _attn_prompt(qt, qit, wit, k_bf, vt, ki_bf, bias_t, bias_far):
    bsz, _, s_len = qt.shape
    assert s_len % (2 * KV_TILE) == 0 and Q_TILE == KV_TILE
    qblk = lambda c: pl.BlockSpec((1, c, Q_TILE), lambda b, i: (b, 0, i))
    return pl.pallas_call(
        functools.partial(_attn_prompt_kernel, k_sel=min(TOPK_MAX, s_len // 4)),
        grid=(bsz, s_len // Q_TILE),
        in_specs=[qblk(ATT_WIDTH), qblk(IDX_HEADS * IDX_DIM), qblk(IDX_HEADS),
                  _resident_spec((1, s_len, ATT_WIDTH), lambda b, i: (b, 0, 0)),
                  _resident_spec((1, ATT_WIDTH, s_len), lambda b, i: (b, 0, 0)),
                  _resident_spec((1, s_len, IDX_DIM), lambda b, i: (b, 0, 0)),
                  _resident_spec(bias_t.shape, lambda b, i: (0, 0, 0, 0)),
                  pl.BlockSpec(memory_space=pltpu.SMEM)],
        out_specs=pl.BlockSpec((1, Q_TILE, ATT_WIDTH), lambda b, i: (b, i, 0)),
        out_shape=jax.ShapeDtypeStruct((bsz, s_len, ATT_WIDTH), BF16),
        scratch_shapes=[pltpu.VMEM((s_len, Q_TILE), I32), pltpu.VMEM((s_len, Q_TILE), jnp.int16),
                        pltpu.VMEM((ATT_WIDTH, Q_TILE), F32),
                        pltpu.VMEM((N_HEADS, 2 * HEAD_DIM, Q_TILE), BF16),
                        pltpu.VMEM((N_HEADS, 1, Q_TILE), F32), pltpu.VMEM((N_HEADS, 1, Q_TILE), F32),
                        pltpu.VMEM((2, N_HEADS, 1, Q_TILE), F32), pltpu.VMEM((2, N_HEADS, 1, Q_TILE), F32),
                        pltpu.VMEM((2, N_HEADS, KV_TILE, Q_TILE), F32)],
        compiler_params=_params(("parallel", "arbitrary")),
        name="attn_prompt",
    )(qt, qit, wit, k_bf, vt, ki_bf, bias_t, bias_far)


def _sample_scores_kernel(pt_ref, qi_ref, wi_ref, kin_ref, *rest, gp):
    pages = rest[:gp]
    keys_o, newkey_o = rest[gp:]
    qi = qi_ref[0]
    wi = wi_ref[0]
    rows = []
    for g in range(gp):
        s = _dot(qi, pages[g][0].astype(BF16))
        rows.append(jnp.sum(wi * jnp.maximum(s, 0.0), axis=0, keepdims=True))
    keys_o[0] = _ordered_key(jnp.concatenate(rows, axis=0))

    @pl.when(pl.program_id(1) == 0)
    def _():
        s_new = jnp.sum(qi.astype(F32) * kin_ref[0].astype(F32), axis=1, keepdims=True)
        key_new = _ordered_key(jnp.sum(wi * jnp.maximum(s_new, 0.0), axis=0, keepdims=True))
        newkey_o[0] = jnp.broadcast_to(key_new, (1, PAGE_SIZE))


def _sample_scores(page_table, qi, wi, ki_new, cache_idx_t):
    n, n_pages = page_table.shape
    gp = min(SAMPLE_IDX_PAGES, n_pages)
    assert n_pages % gp == 0
    page_spec = lambda g: pl.BlockSpec((1, IDX_DIM, PAGE_SIZE), lambda b, c, pt: (pt[b, c * gp + g], 0, 0))
    per_seq = lambda shape: pl.BlockSpec((1,) + shape, lambda b, c, pt: (b, 0, 0))
    grid_spec = pltpu.PrefetchScalarGridSpec(
        num_scalar_prefetch=1,
        grid=(n, n_pages // gp),
        in_specs=[per_seq((IDX_HEADS, IDX_DIM)), per_seq((IDX_HEADS, 1)), per_seq((1, IDX_DIM))]
        + [page_spec(g) for g in range(gp)],
        out_specs=[pl.BlockSpec((1, gp, PAGE_SIZE), lambda b, c, pt: (b, c, 0)), per_seq((1, PAGE_SIZE))],
    )
    return pl.pallas_call(
        functools.partial(_sample_scores_kernel, gp=gp),
        grid_spec=grid_spec,
        out_shape=[jax.ShapeDtypeStruct((n, n_pages, PAGE_SIZE), I32), jax.ShapeDtypeStruct((n, 1, PAGE_SIZE), I32)],
        compiler_params=_params(("parallel", "arbitrary")),
        name="sample_scores",
    )(page_table, qi, wi, ki_new, *([cache_idx_t] * gp))


def _sample_select_kernel(keys_ref, newkey_ref, madd_o, newm_o, *, k_sel):
    n, past = keys_ref.shape
    key_new = newkey_ref[:, :1]
    idx = lax.broadcasted_iota(I32, (n, past), 1)

    def total(ind_past, ind_new):
        return jnp.sum(ind_past.astype(I32), axis=1, keepdims=True) + ind_new.astype(I32)

    lo0 = jnp.full((n, 1), KEY_NEG_INF + 1, I32)
    hi0 = jnp.full((n, 1), KEY_MAX, I32)
    thr, c_thr, c_above = _bisect_kth(lambda mid: total(keys_ref[...] >= mid, key_new >= mid),
                                      lo0, hi0, jnp.full((n, 1), past + 1, I32), jnp.zeros((n, 1), I32), k_sel)
    has_tie = c_thr > k_sel
    need = k_sel - c_above
    tie_j = _bisect_tie_index(
        lambda mid: total(jnp.logical_and(keys_ref[...] == thr, idx <= mid),
                          jnp.logical_and(key_new == thr, past <= mid)),
        need, jnp.max(has_tie.astype(I32)), past, 14)
    tie_j = jnp.where(has_tie, tie_j, KEY_MAX)
    t = keys_ref[...]
    tie_ok = jnp.where(idx <= tie_j, 0.0, NEG_BIG)
    madd_o[...] = jnp.where(t > thr, 0.0, jnp.where(t == thr, tie_ok, NEG_BIG)).astype(F32)
    new_ok = jnp.where(past <= tie_j, 0.0, NEG_BIG)
    newm = jnp.where(key_new > thr, 0.0, jnp.where(key_new == thr, new_ok, NEG_BIG)).astype(F32)
    newm_o[...] = jnp.broadcast_to(newm, (n, PAGE_SIZE))


def _sample_select(keys, newkey):
    n, past = keys.shape
    k_sel = min(TOPK_MAX, (past + 1) // 4)
    return pl.pallas_call(
        functools.partial(_sample_select_kernel, k_sel=k_sel),
        grid=(1,),
        in_specs=[_const_spec(keys.shape), _const_spec(newkey.shape)],
        out_specs=[_const_spec(keys.shape), _const_spec(newkey.shape)],
        out_shape=[jax.ShapeDtypeStruct(keys.shape, F32), jax.ShapeDtypeStruct(newkey.shape, F32)],
        compiler_params=_params(("arbitrary",)),
        name="sample_select",
    )(keys, newkey)


def _attn_sample_kernel(pt_ref, q_ref, kn_ref, vn_ref, madd_ref, newm_ref, bias_ref, biasn_ref, *rest, n_pages):
    gp = SAMPLE_KV_PAGES
    k_pages = rest[:gp]
    v_pages = rest[gp:2 * gp]
    yb_o, m_scr, l_scr, acc_scr = rest[2 * gp:]
    c = pl.program_id(1)
    n_chunks = n_pages // gp
    head_of_lane = lax.broadcasted_iota(I32, (N_HEADS, ATT_WIDTH), 1) // HEAD_DIM
    head_of_row = lax.broadcasted_iota(I32, (N_HEADS, ATT_WIDTH), 0)
    own = head_of_lane == head_of_row
    q_blk = jnp.where(own, jnp.broadcast_to(q_ref[0], (N_HEADS, ATT_WIDTH)), 0.0)

    @pl.when(c == 0)
    def _():
        kn = kn_ref[0].astype(BF16).astype(F32)
        x_new = (jnp.sum(q_blk.astype(BF16).astype(F32) * kn, axis=1, keepdims=True)
                 + biasn_ref[...] + newm_ref[0][:, :1])
        m_scr[...] = x_new
        l_scr[...] = jnp.ones_like(l_scr)
        acc_scr[...] = jnp.broadcast_to(vn_ref[0].astype(BF16).astype(F32), (N_HEADS, ATT_WIDTH))

    q_bf = q_blk.astype(BF16)
    xs = []
    for g in range(gp):
        pg = c * gp + g
        kt = k_pages[g][0].astype(BF16)
        xs.append(_dot(q_bf, kt) + madd_ref[0, pg] + bias_ref[pg])
    x = jnp.concatenate(xs, axis=1)
    m = m_scr[...]
    m_new = jnp.maximum(m, jnp.max(x, axis=1, keepdims=True))
    alpha = jnp.exp2(m - m_new)
    p = jnp.exp2(x - m_new)
    l_scr[...] = alpha * l_scr[...] + jnp.sum(p, axis=1, keepdims=True)
    p_bf = p.astype(BF16)
    pv = jnp.zeros((N_HEADS, ATT_WIDTH), F32)
    for g in range(gp):
        vt = v_pages[g][0].astype(BF16)
        pv = pv + _dot_nt(p_bf[:, g * PAGE_SIZE:(g + 1) * PAGE_SIZE], vt)
    acc_scr[...] = alpha * acc_scr[...] + pv
    m_scr[...] = m_new

    @pl.when(c == n_chunks - 1)
    def _():
        o = acc_scr[...] / l_scr[...]
        yb_o[0] = jnp.sum(jnp.where(own, o, 0.0), axis=0, keepdims=True).astype(BF16)


def _attn_sample(page_table, q, k_new, v_new, madd, newm, bias_pages, bias_new, cache_kt, cache_vt):
    n, n_pages = page_table.shape
    gp = SAMPLE_KV_PAGES
    assert n_pages % gp == 0
    page_spec = lambda g: pl.BlockSpec((1, ATT_WIDTH, PAGE_SIZE), lambda b, c, pt: (pt[b, c * gp + g], 0, 0))
    per_seq = lambda shape: pl.BlockSpec((1,) + shape, lambda b, c, pt: (b, 0, 0))
    grid_spec = pltpu.PrefetchScalarGridSpec(
        num_scalar_prefetch=1,
        grid=(n, n_pages // gp),
        in_specs=[per_seq((1, ATT_WIDTH)), per_seq((1, ATT_WIDTH)), per_seq((1, ATT_WIDTH)),
                  pl.BlockSpec((1, n_pages, 1, PAGE_SIZE), lambda b, c, pt: (b, 0, 0, 0)), per_seq((1, PAGE_SIZE)),
                  pl.BlockSpec(bias_pages.shape, lambda b, c, pt: (0, 0, 0)),
                  pl.BlockSpec(bias_new.shape, lambda b, c, pt: (0, 0))]
        + [page_spec(g) for g in range(gp)] * 2,
        out_specs=per_seq((1, ATT_WIDTH)),
        scratch_shapes=[pltpu.VMEM((N_HEADS, 1), F32), pltpu.VMEM((N_HEADS, 1), F32),
                        pltpu.VMEM((N_HEADS, ATT_WIDTH), F32)],
    )
    return pl.pallas_call(
        functools.partial(_attn_sample_kernel, n_pages=n_pages),
        grid_spec=grid_spec,
        out_shape=jax.ShapeDtypeStruct((n, 1, ATT_WIDTH), BF16),
        compiler_params=_params(("parallel", "arbitrary")),
        name="attn_sample",
    )(page_table, q, k_new, v_new, madd, newm, bias_pages, bias_new, *([cache_kt] * gp), *([cache_vt] * gp))


def _merge_kernel(x_ref, ya_ref, yb_ref, sga_ref, sgb_ref, wa, wb, wo, g_ref, b_ref, x1_o, *, alpha):
    merged = (sga_ref[...].astype(F32) * _dot(ya_ref[...], wa[...])
              + sgb_ref[...].astype(F32) * _dot(yb_ref[...], wb[...]))
    z = alpha * x_ref[...] + _dot(merged.astype(BF16), wo[...])
    x1_o[...] = _layer_norm(z, g_ref[...], b_ref[...])


def _merge(x, ya, yb, sga, sgb, wa, wb, wo, g, b, alpha):
    n = x.shape[0]
    tm = min(TOKEN_TILE, n)
    rows = lambda c: pl.BlockSpec((tm, c), lambda i: (i, 0))
    return pl.pallas_call(
        functools.partial(_merge_kernel, alpha=alpha),
        grid=(n // tm,),
        in_specs=[rows(D_MODEL), rows(SSM_WIDTH), rows(ATT_WIDTH), rows(D_MODEL), rows(D_MODEL),
                  _const_spec(wa.shape), _const_spec(wb.shape), _const_spec(wo.shape),
                  _const_spec(g.shape), _const_spec(b.shape)],
        out_specs=rows(D_MODEL),
        out_shape=jax.ShapeDtypeStruct((n, D_MODEL), F32),
        compiler_params=_params(("parallel",)),
        name="merge_out_proj",
    )(x, ya, yb, sga, sgb, wa, wb, wo, g, b)


def _ffn_kernel(x1_ref, wg, wu, wd, g_ref, b_ref, y_o, acc_scr, *, alpha):
    x1 = x1_ref[...]
    xb = x1.astype(BF16)
    acc_scr[...] = alpha * x1
    for c in range(D_FF // FF_CHUNK):
        sl = slice(c * FF_CHUNK, (c + 1) * FF_CHUNK)
        gate = _dot(xb, wg[:, sl])
        up = _dot(xb, wu[:, sl])
        hid = (jax.nn.silu(gate) * up).astype(BF16)
        acc_scr[...] += _dot(hid, wd[sl, :])
    y_o[...] = _layer_norm(acc_scr[...], g_ref[...], b_ref[...])


def _ffn(x1, wg, wu, wd, g, b, alpha):
    n = x1.shape[0]
    tm = min(TOKEN_TILE, n)
    rows = pl.BlockSpec((tm, D_MODEL), lambda i: (i, 0))
    return pl.pallas_call(
        functools.partial(_ffn_kernel, alpha=alpha),
        grid=(n // tm,),
        in_specs=[rows, _const_spec(wg.shape), _const_spec(wu.shape), _const_spec(wd.shape),
                  _const_spec(g.shape), _const_spec(b.shape)],
        out_specs=rows,
        out_shape=jax.ShapeDtypeStruct((n, D_MODEL), F32),
        scratch_shapes=[pltpu.VMEM((tm, D_MODEL), F32)],
        compiler_params=_params(("parallel",)),
        name="ffn",
    )(x1, wg, wu, wd, g, b)


def _rel_bucket(rel):
    n = jnp.maximum(rel, 0)
    max_exact = REL_BUCKETS // 2
    nf = jnp.maximum(n, max_exact).astype(F32)
    large = max_exact + (jnp.log(nf / max_exact) / math.log(REL_MAX_DIST / max_exact)
                         * (REL_BUCKETS - max_exact)).astype(I32)
    large = jnp.minimum(large, REL_BUCKETS - 1)
    return jnp.where(n < max_exact, n, large)


def _bias_lookup(rel_bias, rel):
    onehot = (_rel_bucket(rel)[..., None] == jnp.arange(REL_BUCKETS, dtype=I32)).astype(F32)
    table = rel_bias.astype(F32) * LOG2E
    return jnp.einsum('...k,kh->...h', onehot, table, precision=lax.Precision.HIGHEST)


def _prompt_bias_tiles(rel_bias):
    kl = jnp.arange(KV_TILE, dtype=I32)[:, None]
    ql = jnp.arange(Q_TILE, dtype=I32)[None, :]
    rel = jnp.stack([ql - kl, ql - kl + KV_TILE])
    assert KV_TILE + 1 >= REL_MAX_DIST
    near = jnp.moveaxis(_bias_lookup(rel_bias, rel), -1, 0)
    far = _bias_lookup(rel_bias, jnp.full((), 2 * KV_TILE, I32))
    return near, far


def _ssm_operators(a_re, a_im, log_dt, b_re, b_im, c_re, c_im, d):
    g, p, gi = SSM_GROUPS, SSM_STATE, SSM_GROUP
    a_re, a_im = a_re.astype(F32), a_im.astype(F32)
    dt = jnp.exp(log_dt.astype(F32))[:, None]
    mag = jnp.exp(dt * a_re)
    lre = mag * jnp.cos(dt * a_im)
    lim = mag * jnp.sin(dt * a_im)
    den = a_re * a_re + a_im * a_im
    fr = ((lre - 1.0) * a_re + lim * a_im) / den
    fi = (lim * a_re - (lre - 1.0) * a_im) / den
    bbr = fr[:, :, None] * b_re - fi[:, :, None] * b_im
    bbi = fr[:, :, None] * b_im + fi[:, :, None] * b_re
    eye = jnp.eye(g, dtype=F32)
    blockdiag_in = lambda m: jnp.einsum('gpi,gh->gihp', m, eye).reshape(g * gi, g * p)
    blockdiag_out = lambda m: jnp.einsum('gip,gh->gphi', m, eye).reshape(g * p, g * gi)
    bfull = jnp.concatenate([blockdiag_in(bbr), blockdiag_in(bbi)], axis=1)
    cfull = jnp.concatenate([blockdiag_out(c_re.astype(F32)), blockdiag_out(-c_im.astype(F32))], axis=0)
    sc = SLAB_COLS
    bslab = jnp.stack([jnp.concatenate([bfull[s * 128:(s + 1) * 128, s * sc:(s + 1) * sc],
                                        bfull[s * 128:(s + 1) * 128, g * p + s * sc:g * p + (s + 1) * sc]], axis=1)
                       for s in range(SSM_SLABS)])
    cslab = jnp.stack([jnp.concatenate([cfull[s * sc:(s + 1) * sc, s * 128:(s + 1) * 128],
                                        cfull[g * p + s * sc:g * p + (s + 1) * sc, s * 128:(s + 1) * 128]], axis=0)
                       for s in range(SSM_SLABS)])
    return {'lre': lre.reshape(1, g * p), 'lim': lim.reshape(1, g * p),
            'bfull': bfull.astype(BF16), 'cfull': cfull.astype(BF16),
            'bslab': bslab.astype(BF16), 'cslab': cslab.astype(BF16), 'd': d.astype(F32).reshape(1, g * gi)}


def _split_w_in(w_in):
    offs = [0]
    for n in IN_SPLITS:
        offs.append(offs[-1] + n)
    names = ['u', 'q', 'k', 'v', 'qi', 'ki', 'wi', 'ga', 'gb']
    return {nm: w_in[:, offs[j]:offs[j + 1]] for j, nm in enumerate(names)}


def kernel(x_prompt, x_sample, cache_k, cache_v, cache_idx_k, state_ssm_re, state_ssm_im, page_table, w_in, ssm_a_re, ssm_a_im, ssm_log_dt, ssm_b_re, ssm_b_im, ssm_c_re, ssm_c_im, ssm_d, w_glu, w_branch_a, w_branch_b, rel_bias, w_out, ln1_g, ln1_b, w_ffn_in, w_ffn_out, ln2_g, ln2_b):
    depth = w_in.shape[0]
    bsz, s_len, _ = x_prompt.shape
    n_dec, dec_seq, _ = x_sample.shape
    assert dec_seq == 1
    n_pages = page_table.shape[1]
    n_pool = cache_k.shape[1]
    past = n_pages * PAGE_SIZE
    alpha = (2 * depth) ** 0.25
    idx_scale = IDX_DIM ** -0.5 * IDX_HEADS ** -0.5
    att_scale = HEAD_DIM ** -0.5 * LOG2E

    bias_t, bias_far = _prompt_bias_tiles(rel_bias)
    key_pos = jnp.arange(past, dtype=I32).reshape(n_pages, PAGE_SIZE)
    bias_pages = jnp.swapaxes(_bias_lookup(rel_bias, past - key_pos), 1, 2)
    bias_new = _bias_lookup(rel_bias, jnp.zeros((), I32)).reshape(N_HEADS, 1)

    x_p = x_prompt
    x_s = x_sample.reshape(n_dec, D_MODEL)
    outs = {k: [] for k in ('kp', 'vp', 'kip', 'hpr', 'hpi', 'ks', 'vs', 'kis', 'hsr', 'hsi')}
    for l in range(depth):
        w = _split_w_in(w_in[l])
        wn = {nm: w[nm].astype(BF16) for nm in ('u', 'k', 'v', 'ki', 'ga', 'gb')}
        wn['tq'] = (w['q'] * att_scale).T.astype(BF16)
        wn['tv'] = w['v'].T.astype(BF16)
        wn['tqi'] = w['qi'].T.astype(BF16)
        wn['twi'] = jnp.pad((w['wi'] * idx_scale).T, ((0, 16 - IDX_HEADS), (0, 0))).astype(BF16)
        ws = {nm: w[nm].astype(BF16) for nm in ('u', 'k', 'v', 'qi', 'ki', 'ga', 'gb')}
        ws['q'] = (w['q'] * att_scale).astype(BF16)
        ws['wi'] = (w['wi'] * idx_scale).astype(BF16)
        ops = _ssm_operators(ssm_a_re[l], ssm_a_im[l], ssm_log_dt[l], ssm_b_re[l], ssm_b_im[l],
                             ssm_c_re[l], ssm_c_im[l], ssm_d[l])
        wglu = w_glu[l].astype(BF16)
        wa, wb, wo = w_branch_a[l].astype(BF16), w_branch_b[l].astype(BF16), w_out[l].astype(BF16)
        wg, wu = w_ffn_in[l][:, :D_FF].astype(BF16), w_ffn_in[l][:, D_FF:].astype(BF16)
        wd = w_ffn_out[l].astype(BF16)
        g1, b1 = ln1_g[l].astype(F32).reshape(1, D_MODEL), ln1_b[l].astype(F32).reshape(1, D_MODEL)
        g2, b2 = ln2_g[l].astype(F32).reshape(1, D_MODEL), ln2_b[l].astype(F32).reshape(1, D_MODEL)

        u, k, v, ki, k_bf, ki_bf, sga, sgb, qt, vt, qit, wit = _in_proj_prompt(x_p, wn)
        ya, hpr, hpi = _ssm_prompt(u, ops, wglu)
        yb = _attn_prompt(qt, qit, wit, k_bf, vt, ki_bf, bias_t, bias_far)
        n_tok = bsz * s_len
        flat = lambda a: a.reshape(n_tok, a.shape[-1])
        x1 = _merge(flat(x_p), flat(ya), flat(yb), flat(sga), flat(sgb), wa, wb, wo, g1, b1, alpha)
        x_p = _ffn(x1, wg, wu, wd, g2, b2, alpha).reshape(bsz, s_len, D_MODEL)
        outs['kp'].append(k.reshape(bsz, s_len, N_HEADS, HEAD_DIM))
        outs['vp'].append(v.reshape(bsz, s_len, N_HEADS, HEAD_DIM))
        outs['kip'].append(ki)
        outs['hpr'].append(hpr.reshape(bsz, SSM_GROUPS, SSM_STATE))
        outs['hpi'].append(hpi.reshape(bsz, SSM_GROUPS, SSM_STATE))

        us, qs, k_s, v_s, qis, kis, wis, sga_s, sgb_s = _in_proj_sample(x_s, ws)
        ya_s, hsr, hsi = _ssm_sample(us, state_ssm_re[l].reshape(n_dec, SSM_COLS).astype(F32),
                                     state_ssm_im[l].reshape(n_dec, SSM_COLS).astype(F32), ops, wglu)
        cache_kt = jnp.transpose(cache_k[l], (0, 2, 3, 1)).reshape(n_pool, ATT_WIDTH, PAGE_SIZE)
        cache_vt = jnp.transpose(cache_v[l], (0, 2, 3, 1)).reshape(n_pool, ATT_WIDTH, PAGE_SIZE)
        cache_it = jnp.transpose(cache_idx_k[l], (0, 2, 1))
        keys, newkey = _sample_scores(
            page_table, qis.astype(BF16).reshape(n_dec, IDX_HEADS, IDX_DIM), wis.reshape(n_dec, IDX_HEADS, 1),
            kis.astype(BF16).reshape(n_dec, 1, IDX_DIM), cache_it)
        madd, newm = _sample_select(keys.reshape(n_dec, past), newkey.reshape(n_dec, PAGE_SIZE))
        yb_s = _attn_sample(page_table, qs.reshape(n_dec, 1, ATT_WIDTH), k_s.reshape(n_dec, 1, ATT_WIDTH),
                            v_s.reshape(n_dec, 1, ATT_WIDTH), madd.reshape(n_dec, n_pages, 1, PAGE_SIZE),
                            newm.reshape(n_dec, 1, PAGE_SIZE), bias_pages, bias_new, cache_kt, cache_vt)
        x1_s = _merge(x_s, ya_s, yb_s.reshape(n_dec, ATT_WIDTH), sga_s, sgb_s, wa, wb, wo, g1, b1, alpha)
        x_s = _ffn(x1_s, wg, wu, wd, g2, b2, alpha)
        outs['ks'].append(k_s.reshape(n_dec, 1, N_HEADS, HEAD_DIM))
        outs['vs'].append(v_s.reshape(n_dec, 1, N_HEADS, HEAD_DIM))
        outs['kis'].append(kis.reshape(n_dec, 1, IDX_DIM))
        outs['hsr'].append(hsr.reshape(n_dec, SSM_GROUPS, SSM_STATE))
        outs['hsi'].append(hsi.reshape(n_dec, SSM_GROUPS, SSM_STATE))

    st = lambda name: jnp.stack(outs[name])
    return (x_p, x_s.reshape(n_dec, 1, D_MODEL),
            st('kp'), st('vp'), st('kip'), st('hpr'), st('hpi'),
            st('ks'), st('vs'), st('kis'), st('hsr'), st('hsi'))
```

```python
import functools
import math

import jax
import jax.numpy as jnp
from jax import lax
from jax.experimental import pallas as pl
from jax.experimental.pallas import tpu as pltpu

F32 = jnp.float32
BF16 = jnp.bfloat16
I32 = jnp.int32

D_MODEL = 1024
SSM_WIDTH = 512
SSM_GROUP = 16
SSM_GROUPS = 32
SSM_STATE = 64
SSM_COLS = SSM_GROUPS * SSM_STATE
N_HEADS = 8
HEAD_DIM = 64
ATT_WIDTH = 512
IDX_HEADS = 8
IDX_DIM = 32
TOPK_MAX = 256
PAGE_SIZE = 128
REL_BUCKETS = 32
REL_MAX_DIST = 128
D_FF = 2816
LN_EPS = 1e-5
IN_SPLITS = (SSM_WIDTH, ATT_WIDTH, ATT_WIDTH, ATT_WIDTH, IDX_HEADS * IDX_DIM, IDX_DIM, IDX_HEADS, D_MODEL, D_MODEL)
LOG2E = math.log2(math.e)

VMEM_LIMIT_BYTES = 56 * 1024 * 1024
NEG_BIG = -1e30
KEY_NEG_INF = -2139095041
KEY_MAX = 2147483647

TOKEN_TILE = 512
Q_TILE = 256
KV_TILE = 256
SSM_T_TILE = 64
FF_CHUNK = 256
SAMPLE_IDX_PAGES = 64
SAMPLE_KV_PAGES = 32


def _params(sem):
    return pltpu.CompilerParams(dimension_semantics=sem, vmem_limit_bytes=VMEM_LIMIT_BYTES)


def _const_spec(shape):
    n = len(shape)
    return pl.BlockSpec(shape, lambda *_: (0,) * n)


def _resident_spec(block, index_map):
    return pl.BlockSpec(block, index_map, pipeline_mode=pl.Buffered(1))


def _dot(a, b):
    return jnp.dot(a, b, preferred_element_type=F32)


def _dot_nt(a, b):
    return lax.dot_general(a, b, (((1,), (1,)), ((), ())), preferred_element_type=F32)


def _layer_norm(z, g, b):
    mu = jnp.mean(z, axis=-1, keepdims=True)
    zc = z - mu
    var = jnp.mean(zc * zc, axis=-1, keepdims=True)
    return zc * lax.rsqrt(var + LN_EPS) * g + b


def _ordered_key(x):
    b = pltpu.bitcast(x, I32)
    return b ^ ((b >> 31) & KEY_MAX)


def _in_proj_prompt_kernel(x_ref, wu, wk, wv, wga, wgb, wki, wtq, wtv, wtqi, wtwi,
                           u_o, k_o, v_o, ki_o, kbf_o, kibf_o, sga_o, sgb_o, qt_o, vt_o, qit_o, wit_o):
    xb = x_ref[0].astype(BF16)
    u_o[0] = _dot(xb, wu[...])
    k = _dot(xb, wk[...])
    k_o[0] = k
    kbf_o[0] = k.astype(BF16)
    v_o[0] = _dot(xb, wv[...])
    ki = _dot(xb, wki[...])
    ki_o[0] = ki
    kibf_o[0] = ki.astype(BF16)
    sga_o[0] = jax.nn.sigmoid(_dot(xb, wga[...])).astype(BF16)
    sgb_o[0] = jax.nn.sigmoid(_dot(xb, wgb[...])).astype(BF16)
    qt_o[0] = _dot_nt(wtq[...], xb).astype(BF16)
    vt_o[0] = _dot_nt(wtv[...], xb).astype(BF16)
    qit_o[0] = _dot_nt(wtqi[...], xb).astype(BF16)
    wit_o[0] = _dot_nt(wtwi[...], xb)[:IDX_HEADS]


def _in_proj_prompt(x, w):
    bsz, s_len, _ = x.shape
    tm = min(TOKEN_TILE, s_len)
    nat = lambda c: pl.BlockSpec((1, tm, c), lambda b, i: (b, i, 0))
    tr = lambda c: pl.BlockSpec((1, c, tm), lambda b, i: (b, 0, i))
    sds = jax.ShapeDtypeStruct
    weights = [w['u'], w['k'], w['v'], w['ga'], w['gb'], w['ki'], w['tq'], w['tv'], w['tqi'], w['twi']]
    return pl.pallas_call(
        _in_proj_prompt_kernel,
        grid=(bsz, s_len // tm),
        in_specs=[nat(D_MODEL)] + [_const_spec(a.shape) for a in weights],
        out_specs=[nat(SSM_WIDTH), nat(ATT_WIDTH), nat(ATT_WIDTH), nat(IDX_DIM), nat(ATT_WIDTH), nat(IDX_DIM),
                   nat(D_MODEL), nat(D_MODEL), tr(ATT_WIDTH), tr(ATT_WIDTH), tr(IDX_HEADS * IDX_DIM), tr(IDX_HEADS)],
        out_shape=[sds((bsz, s_len, SSM_WIDTH), F32), sds((bsz, s_len, ATT_WIDTH), F32),
                   sds((bsz, s_len, ATT_WIDTH), F32), sds((bsz, s_len, IDX_DIM), F32),
                   sds((bsz, s_len, ATT_WIDTH), BF16), sds((bsz, s_len, IDX_DIM), BF16),
                   sds((bsz, s_len, D_MODEL), BF16), sds((bsz, s_len, D_MODEL), BF16),
                   sds((bsz, ATT_WIDTH, s_len), BF16), sds((bsz, ATT_WIDTH, s_len), BF16),
                   sds((bsz, IDX_HEADS * IDX_DIM, s_len), BF16), sds((bsz, IDX_HEADS, s_len), F32)],
        compiler_params=_params(("parallel", "parallel")),
        name="in_proj_prompt",
    )(x, *weights)


def _in_proj_sample_kernel(x_ref, wu, wq, wk, wv, wqi, wki, wwi, wga, wgb,
                           u_o, q_o, k_o, v_o, qi_o, ki_o, wi_o, sga_o, sgb_o):
    xb = x_ref[...].astype(BF16)
    u_o[...] = _dot(xb, wu[...])
    q_o[...] = _dot(xb, wq[...])
    k_o[...] = _dot(xb, wk[...])
    v_o[...] = _dot(xb, wv[...])
    qi_o[...] = _dot(xb, wqi[...])
    ki_o[...] = _dot(xb, wki[...])
    wi_o[...] = _dot(xb, wwi[...])
    sga_o[...] = jax.nn.sigmoid(_dot(xb, wga[...])).astype(BF16)
    sgb_o[...] = jax.nn.sigmoid(_dot(xb, wgb[...])).astype(BF16)


def _in_proj_sample(x, w):
    n = x.shape[0]
    weights = [w['u'], w['q'], w['k'], w['v'], w['qi'], w['ki'], w['wi'], w['ga'], w['gb']]
    cols = [SSM_WIDTH, ATT_WIDTH, ATT_WIDTH, ATT_WIDTH, IDX_HEADS * IDX_DIM, IDX_DIM, IDX_HEADS, D_MODEL, D_MODEL]
    dts = [F32] * 7 + [BF16, BF16]
    return pl.pallas_call(
        _in_proj_sample_kernel,
        grid=(1,),
        in_specs=[_const_spec(x.shape)] + [_const_spec(a.shape) for a in weights],
        out_specs=[_const_spec((n, c)) for c in cols],
        out_shape=[jax.ShapeDtypeStruct((n, c), d) for c, d in zip(cols, dts)],
        compiler_params=_params(("arbitrary",)),
        name="in_proj_sample",
    )(x, *weights)


def _glu_gate(y, wglu):
    y = jax.nn.gelu(y)
    return y * jax.nn.sigmoid(_dot(y.astype(BF16), wglu[...]))


def _glu_tail(h_bf, u, cfull, dvec, wglu):
    return _glu_gate(_dot(h_bf, cfull[...]) + dvec[...] * u, wglu)


SSM_SLABS = SSM_WIDTH // 128
SLAB_COLS = SSM_COLS // SSM_SLABS


def _ssm_prompt_kernel(u_ref, perm_ref, permt_ref, bslab, cslab, lre_ref, lim_ref, dvec, wglu,
                       ya_o, hre_o, him_o, h_scr, u_scr, y_scr, state_scr, *, bsz, tt):
    c = pl.program_id(0)
    rows = bsz * tt
    row_chunk = 256

    @pl.when(c == 0)
    def _():
        state_scr[...] = jnp.zeros_like(state_scr)

    u_bt = jnp.concatenate([u_ref[b] for b in range(bsz)], axis=0)
    u_hi = u_bt.astype(BF16)
    u_lo = (u_bt - u_hi.astype(F32)).astype(BF16)
    u_scr[...] = _dot(perm_ref[...], u_hi) + _dot(perm_ref[...], u_lo)
    for r in range(rows // row_chunk):
        sl = slice(r * row_chunk, (r + 1) * row_chunk)
        u_bf = u_scr[sl, :].astype(BF16)
        for s in range(SSM_SLABS):
            bu = _dot(u_bf[:, s * 128:(s + 1) * 128], bslab[s])
            h_scr[sl, s * SLAB_COLS:(s + 1) * SLAB_COLS] = bu[:, :SLAB_COLS]
            h_scr[sl, SSM_COLS + s * SLAB_COLS:SSM_COLS + (s + 1) * SLAB_COLS] = bu[:, SLAB_COLS:]

    chunk = 512
    for ci in range(SSM_COLS // chunk):
        re_sl = slice(ci * chunk, (ci + 1) * chunk)
        im_sl = slice(SSM_COLS + ci * chunk, SSM_COLS + (ci + 1) * chunk)
        lre = lre_ref[:, re_sl]
        lim = lim_ref[:, re_sl]

        def step(t, carry):
            hr, hi = carry
            r0 = pl.multiple_of(t * bsz, bsz)
            nr = lre * hr - lim * hi + h_scr[pl.ds(r0, bsz), re_sl]
            ni = lre * hi + lim * hr + h_scr[pl.ds(r0, bsz), im_sl]
            h_scr[pl.ds(r0, bsz), re_sl] = nr
            h_scr[pl.ds(r0, bsz), im_sl] = ni
            return nr, ni

        hr, hi = lax.fori_loop(0, tt, step, (state_scr[:, re_sl], state_scr[:, im_sl]), unroll=4)
        state_scr[:, re_sl] = hr
        state_scr[:, im_sl] = hi

    for r in range(rows // row_chunk):
        sl = slice(r * row_chunk, (r + 1) * row_chunk)
        ys = []
        for s in range(SSM_SLABS):
            h_re = h_scr[sl, s * SLAB_COLS:(s + 1) * SLAB_COLS].astype(BF16)
            h_im = h_scr[sl, SSM_COLS + s * SLAB_COLS:SSM_COLS + (s + 1) * SLAB_COLS].astype(BF16)
            ys.append(_dot(h_re, cslab[s, :SLAB_COLS]) + _dot(h_im, cslab[s, SLAB_COLS:]))
        y = jnp.concatenate(ys, axis=1) + dvec[...] * u_scr[sl, :]
        y_scr[sl, :] = _glu_gate(y, wglu).astype(BF16)

    ya_bt = _dot(permt_ref[...], y_scr[...]).astype(BF16)
    for b in range(bsz):
        ya_o[b] = ya_bt[b * tt:(b + 1) * tt]

    @pl.when(c == pl.num_programs(0) - 1)
    def _():
        hre_o[...] = state_scr[:, :SSM_COLS]
        him_o[...] = state_scr[:, SSM_COLS:]


def _ssm_prompt(u, ops, wglu):
    bsz, s_len, _ = u.shape
    tt = min(SSM_T_TILE, s_len)
    rows = bsz * tt
    lre = jnp.broadcast_to(ops['lre'], (bsz, SSM_COLS))
    lim = jnp.broadcast_to(ops['lim'], (bsz, SSM_COLS))
    kern = functools.partial(_ssm_prompt_kernel, bsz=bsz, tt=tt)
    src = (jnp.arange(rows, dtype=I32) % bsz) * tt + jnp.arange(rows, dtype=I32) // bsz
    perm = (src[:, None] == jnp.arange(rows, dtype=I32)[None, :]).astype(BF16)
    return pl.pallas_call(
        kern,
        grid=(s_len // tt,),
        in_specs=[pl.BlockSpec((bsz, tt, SSM_WIDTH), lambda c: (0, c, 0)),
                  _const_spec(perm.shape), _const_spec(perm.shape),
                  _const_spec(ops['bslab'].shape), _const_spec(ops['cslab'].shape),
                  _const_spec(lre.shape), _const_spec(lim.shape), _const_spec(ops['d'].shape),
                  _const_spec(wglu.shape)],
        out_specs=[pl.BlockSpec((bsz, tt, SSM_WIDTH), lambda c: (0, c, 0)),
                   _const_spec((bsz, SSM_COLS)), _const_spec((bsz, SSM_COLS))],
        out_shape=[jax.ShapeDtypeStruct((bsz, s_len, SSM_WIDTH), BF16),
                   jax.ShapeDtypeStruct((bsz, SSM_COLS), F32), jax.ShapeDtypeStruct((bsz, SSM_COLS), F32)],
        scratch_shapes=[pltpu.VMEM((rows, 2 * SSM_COLS), F32), pltpu.VMEM((rows, SSM_WIDTH), F32),
                        pltpu.VMEM((rows, SSM_WIDTH), BF16), pltpu.VMEM((bsz, 2 * SSM_COLS), F32)],
        compiler_params=_params(("arbitrary",)),
        name="ssm_prompt",
    )(u, perm, perm.T, ops['bslab'], ops['cslab'], lre, lim, ops['d'], wglu)


def _ssm_sample_kernel(u_ref, h0re_ref, h0im_ref, bfull, cfull, lre_ref, lim_ref, dvec, wglu,
                       ya_o, hre_o, him_o):
    u = u_ref[...]
    bu = _dot(u.astype(BF16), bfull[...])
    lre = lre_ref[...]
    lim = lim_ref[...]
    h0r = h0re_ref[...]
    h0i = h0im_ref[...]
    hr = lre * h0r - lim * h0i + bu[:, :SSM_COLS]
    hi = lre * h0i + lim * h0r + bu[:, SSM_COLS:]
    hre_o[...] = hr
    him_o[...] = hi
    h_bf = jnp.concatenate([hr.astype(BF16), hi.astype(BF16)], axis=1)
    ya_o[...] = _glu_tail(h_bf, u, cfull, dvec, wglu).astype(BF16)


def _ssm_sample(u, h0re, h0im, ops, wglu):
    n = u.shape[0]
    args = [u, h0re, h0im, ops['bfull'], ops['cfull'], ops['lre'], ops['lim'], ops['d'], wglu]
    return pl.pallas_call(
        _ssm_sample_kernel,
        grid=(1,),
        in_specs=[_const_spec(a.shape) for a in args],
        out_specs=[_const_spec((n, SSM_WIDTH)), _const_spec((n, SSM_COLS)), _const_spec((n, SSM_COLS))],
        out_shape=[jax.ShapeDtypeStruct((n, SSM_WIDTH), BF16),
                   jax.ShapeDtypeStruct((n, SSM_COLS), F32), jax.ShapeDtypeStruct((n, SSM_COLS), F32)],
        compiler_params=_params(("arbitrary",)),
        name="ssm_sample",
    )(*args)


def _bisect_step(count_ge, k, st):
    lo, hi, c_lo, c_hi = st
    mid = (lo >> 1) + (hi >> 1) + (lo & hi & 1)
    cnt = count_ge(mid)
    active = lo + 1 != hi
    up = jnp.logical_and(active, cnt >= k)
    down = jnp.logical_and(active, cnt < k)
    n_lo = jnp.where(up, mid, lo)
    n_clo = jnp.where(up, cnt, c_lo)
    n_hi = jnp.where(down, mid, jnp.where(jnp.logical_and(active, cnt == k), mid + 1, hi))
    n_chi = jnp.where(down, cnt, c_hi)
    return n_lo, n_hi, n_clo, n_chi


def _bisect_kth(count_ge, lo, hi, c_lo, c_hi, k, steps_per_check=1):
    def cond(st):
        it, lo, hi, _, _ = st
        return jnp.logical_and(it < 34, jnp.max((lo + 1 != hi).astype(I32)) > 0)

    def body(st):
        it, st = st[0], st[1:]
        for _ in range(steps_per_check):
            st = _bisect_step(count_ge, k, st)
        return (it + steps_per_check,) + st

    _, lo, hi, c_lo, c_hi = lax.while_loop(cond, body, (jnp.int32(0), lo, hi, c_lo, c_hi))
    return lo, c_lo, c_hi


def _bisect_kth_fixed(count_ge, lo, hi, c_lo, c_hi, k):
    width = hi - lo
    n_steps = jnp.max(32 - lax.clz(width - 1))
    lo, hi, c_lo, c_hi = lax.fori_loop(0, n_steps, lambda _, st: _bisect_step(count_ge, k, st),
                                       (lo, hi, c_lo, c_hi))
    return lo, c_lo, c_hi


def _bisect_tie_index(count_tie_le, need, n_tie_rows, idx_hi, n_bits):
    def cond(st):
        it, _, _ = st
        return jnp.logical_and(it < n_bits, n_tie_rows > 0)

    def body(st):
        it, lo, hi = st
        mid = (lo + hi) >> 1
        ok = count_tie_le(mid) >= need
        return it + 1, jnp.where(ok, lo, mid), jnp.where(ok, mid, hi)

    lo0 = jnp.full_like(need, -1)
    hi0 = jnp.full_like(need, idx_hi)
    _, _, hi = lax.while_loop(cond, body, (jnp.int32(0), lo0, hi0))
    return hi


def _attn_prompt_kernel(qt_ref, qit_ref, wit_ref, k_ref, vt_ref, ki_ref, bias_ref, bfar_ref, yb_o,
                        s_scr, s16_scr, o_scr, qm_scr, m_scr, l_scr, a_scr, off_scr, x_scr, *, k_sel):
    i = pl.program_id(1)
    n_tiles = i + 1
    tq = Q_TILE
    tk = KV_TILE
    q_pos = i * tq + lax.broadcasted_iota(I32, (1, tq), 1)
    kv_iota = lax.broadcasted_iota(I32, (tk, tq), 0)

    n_pairs = (n_tiles + 1) // 2

    def score_pair(jj, top):
        for half in range(2):
            r0 = pl.multiple_of((2 * jj + half) * tk, tk)
            ki_t = ki_ref[0, pl.ds(r0, tk), :]
            acc = jnp.zeros((tk, tq), F32)
            for h in range(IDX_HEADS):
                s = _dot(ki_t, qit_ref[0, h * IDX_DIM:(h + 1) * IDX_DIM, :])
                acc = acc + wit_ref[0, h:h + 1, :] * jnp.maximum(s, 0.0)
            acc = jnp.where(kv_iota + r0 <= q_pos, acc, -jnp.inf)
            key = _ordered_key(acc)
            s_scr[pl.ds(r0, tk), :] = key
            s16_scr[pl.ds(r0, tk), :] = (key >> 16).astype(jnp.int16)
            top = jnp.maximum(top, jnp.max(acc, axis=0, keepdims=True))
        return top

    top = lax.fori_loop(0, n_pairs, score_pair, jnp.full((1, tq), -jnp.inf, F32))

    def count_tiles(pred):
        def tile(jj, acc):
            for half in range(2):
                r0 = pl.multiple_of((2 * jj + half) * tk, tk)
                ind = pred(s_scr[pl.ds(r0, tk), :], r0).astype(I32)
                for r in range(tk // 8):
                    acc = acc + ind[r * 8:(r + 1) * 8]
            return acc
        acc = lax.fori_loop(0, n_pairs, tile, jnp.zeros((8, tq), I32))
        return jnp.sum(acc, axis=0, keepdims=True)

    def count_ge_high(mid):
        mid16 = mid.astype(jnp.int16)

        def tile(jj, acc):
            r0 = pl.multiple_of(jj * 2 * tk, 2 * tk)
            ind = jnp.where(s16_scr[pl.ds(r0, 2 * tk), :] >= mid16, jnp.int16(1), jnp.int16(0))
            for r in range(2 * tk // 16):
                acc = acc + ind[r * 16:(r + 1) * 16]
            return acc
        acc = lax.fori_loop(0, n_pairs, tile, jnp.zeros((16, tq), jnp.int16))
        return jnp.sum(acc.astype(I32), axis=0, keepdims=True)

    count_ge = lambda mid: count_tiles(lambda t, r0: t >= mid)
    n_valid = q_pos + 1
    searching = n_valid > k_sel
    zeros = jnp.zeros((1, tq), I32)
    floor_high = jnp.full((1, tq), (KEY_NEG_INF + 1) >> 16, I32)
    top_high = _ordered_key(top) >> 16
    guess = jnp.maximum(top_high - (4 << 7), 0)
    c_guess = count_ge_high(guess)
    use_guess = jnp.logical_and(jnp.logical_and(searching, top_high > 0), c_guess >= k_sel)
    lo_high = jnp.where(use_guess, guess, floor_high)
    c_lo_high = jnp.where(use_guess, c_guess, n_valid)
    hi_high = jnp.where(searching, top_high + 1, floor_high + 1)
    bucket, c_bucket, c_next = _bisect_kth_fixed(count_ge_high, lo_high, hi_high, c_lo_high, zeros, k_sel)
    settled = jnp.logical_or(jnp.logical_not(searching), c_bucket == k_sel)
    lo0 = bucket << 16
    bucket_end = jnp.where(bucket >= 32767, KEY_MAX, (bucket + 1) << 16)
    hi0 = jnp.where(settled, lo0 + 1, bucket_end)
    thr, c_thr, c_above = _bisect_kth(count_ge, lo0, hi0, c_bucket, c_next, k_sel, steps_per_check=2)
    has_tie = jnp.logical_and(searching, c_thr > k_sel)
    need = k_sel - c_above
    tie_j = _bisect_tie_index(
        lambda mid: count_tiles(lambda t, r0: jnp.logical_and(t == thr, kv_iota + r0 <= mid)),
        need, jnp.max(has_tie.astype(I32)), n_tiles * tk - 1, 14)
    tie_j = jnp.where(has_tie, tie_j, KEY_MAX)

    def mask_pair(jj, _):
        for half in range(2):
            r0 = pl.multiple_of((2 * jj + half) * tk, tk)
            t = s_scr[pl.ds(r0, tk), :]
            tie_ok = jnp.where(kv_iota + r0 <= tie_j, 0.0, NEG_BIG)
            madd = jnp.where(t > thr, 0.0, jnp.where(t == thr, tie_ok, NEG_BIG))
            s_scr[pl.ds(r0, tk), :] = pltpu.bitcast(madd.astype(F32), I32)
        return 0

    lax.fori_loop(0, n_pairs, mask_pair, 0)

    row = lax.broadcasted_iota(I32, (2 * HEAD_DIM, tq), 0)
    for pair in range(N_HEADS // 2):
        q_pair = qt_ref[0, pair * 128:(pair + 1) * 128, :]
        zero = jnp.zeros_like(q_pair)
        qm_scr[2 * pair] = jnp.where(row < HEAD_DIM, q_pair, zero)
        qm_scr[2 * pair + 1] = jnp.where(row >= HEAD_DIM, q_pair, zero)
    m_scr[...] = jnp.full(m_scr.shape, NEG_BIG, F32)
    l_scr[...] = jnp.zeros(l_scr.shape, F32)
    o_scr[...] = jnp.zeros(o_scr.shape, F32)

    def attn_tiles(tiles):
        for slot, (j, near_kind) in enumerate(tiles):
            r0 = pl.multiple_of(j * tk, tk)
            for pair in range(N_HEADS // 2):
                k_t = k_ref[0, pl.ds(r0, tk), pair * 128:(pair + 1) * 128]
                for h in (2 * pair, 2 * pair + 1):
                    x = _dot(k_t, qm_scr[h]) + pltpu.bitcast(s_scr[pl.ds(r0, tk), :], F32)
                    if near_kind is None:
                        c = bfar_ref[h]
                    else:
                        x = x + bias_ref[h, near_kind]
                        c = 0.0
                    x_scr[slot, h] = x
                    m = m_scr[h]
                    m_new = jnp.maximum(m, jnp.max(x, axis=0, keepdims=True) + c)
                    a_scr[slot, h] = jnp.exp2(m - m_new)
                    off_scr[slot, h] = m_new - c
                    m_scr[h] = m_new
        for slot, (j, _) in enumerate(tiles):
            r0 = pl.multiple_of(j * tk, tk)
            for h in range(N_HEADS):
                alpha = a_scr[slot, h]
                p = jnp.exp2(x_scr[slot, h] - off_scr[slot, h])
                l_scr[h] = alpha * l_scr[h] + jnp.sum(p, axis=0, keepdims=True)
                v_t = vt_ref[0, h * HEAD_DIM:(h + 1) * HEAD_DIM, pl.ds(r0, tk)]
                hs = slice(h * HEAD_DIM, (h + 1) * HEAD_DIM)
                o_scr[hs, :] = alpha * o_scr[hs, :] + _dot(v_t, p.astype(BF16))

    n_far = jnp.maximum(i - 1, 0)

    def far_pair(jj, _):
        attn_tiles([(2 * jj, None), (2 * jj + 1, None)])
        return 0

    lax.fori_loop(0, n_far // 2, far_pair, 0)

    @pl.when(n_far % 2 == 1)
    def _():
        attn_tiles([(n_far - 1, None)])

    @pl.when(i >= 1)
    def _():
        attn_tiles([(i - 1, 1), (i, 0)])

    @pl.when(i == 0)
    def _():
        attn_tiles([(i, 0)])

    for h in range(N_HEADS):
        hs = slice(h * HEAD_DIM, (h + 1) * HEAD_DIM)
        o_scr[hs, :] = o_scr[hs, :] / l_scr[h]
    yb_o[0] = o_scr[...].T.astype(BF16)


def _attn_prompt(qt, qit, wit, k_bf, vt, ki_bf, bias_t, bias_far):
    bsz, _, s_len = qt.shape
    assert s_len % (2 * KV_TILE) == 0 and Q_TILE == KV_TILE
    qblk = lambda c: pl.BlockSpec((1, c, Q_TILE), lambda b, i: (b, 0, i))
    return pl.pallas_call(
        functools.partial(_attn_prompt_kernel, k_sel=min(TOPK_MAX, s_len // 4)),
        grid=(bsz, s_len // Q_TILE),
        in_specs=[qblk(ATT_WIDTH), qblk(IDX_HEADS * IDX_DIM), qblk(IDX_HEADS),
                  _resident_spec((1, s_len, ATT_WIDTH), lambda b, i: (b, 0, 0)),
                  _resident_spec((1, ATT_WIDTH, s_len), lambda b, i: (b, 0, 0)),
                  _resident_spec((1, s_len, IDX_DIM), lambda b, i: (b, 0, 0)),
                  _resident_spec(bias_t.shape, lambda b, i: (0, 0, 0, 0)),
                  pl.BlockSpec(memory_space=pltpu.SMEM)],
        out_specs=pl.BlockSpec((1, Q_TILE, ATT_WIDTH), lambda b, i: (b, i, 0)),
        out_shape=jax.ShapeDtypeStruct((bsz, s_len, ATT_WIDTH), BF16),
        scratch_shapes=[pltpu.VMEM((s_len, Q_TILE), I32), pltpu.VMEM((s_len, Q_TILE), jnp.int16),
                        pltpu.VMEM((ATT_WIDTH, Q_TILE), F32),
                        pltpu.VMEM((N_HEADS, 2 * HEAD_DIM, Q_TILE), BF16),
                        pltpu.VMEM((N_HEADS, 1, Q_TILE), F32), pltpu.VMEM((N_HEADS, 1, Q_TILE), F32),
                        pltpu.VMEM((2, N_HEADS, 1, Q_TILE), F32), pltpu.VMEM((2, N_HEADS, 1, Q_TILE), F32),
                        pltpu.VMEM((2, N_HEADS, KV_TILE, Q_TILE), F32)],
        compiler_params=_params(("parallel", "arbitrary")),
        name="attn_prompt",
    )(qt, qit, wit, k_bf, vt, ki_bf, bias_t, bias_far)


def _sample_scores_kernel(pt_ref, qi_ref, wi_ref, kin_ref, *rest, gp):
    pages = rest[:gp]
    keys_o, newkey_o = rest[gp:]
    qi = qi_ref[0]
    wi = wi_ref[0]
    rows = []
    for g in range(gp):
        s = _dot(qi, pages[g][0].astype(BF16))
        rows.append(jnp.sum(wi * jnp.maximum(s, 0.0), axis=0, keepdims=True))
    keys_o[0] = _ordered_key(jnp.concatenate(rows, axis=0))

    @pl.when(pl.program_id(1) == 0)
    def _():
        s_new = jnp.sum(qi.astype(F32) * kin_ref[0].astype(F32), axis=1, keepdims=True)
        key_new = _ordered_key(jnp.sum(wi * jnp.maximum(s_new, 0.0), axis=0, keepdims=True))
        newkey_o[0] = jnp.broadcast_to(key_new, (1, PAGE_SIZE))


def _sample_scores(page_table, qi, wi, ki_new, cache_idx_t):
    n, n_pages = page_table.shape
    gp = min(SAMPLE_IDX_PAGES, n_pages)
    assert n_pages % gp == 0
    page_spec = lambda g: pl.BlockSpec((1, IDX_DIM, PAGE_SIZE), lambda b, c, pt: (pt[b, c * gp + g], 0, 0))
    per_seq = lambda shape: pl.BlockSpec((1,) + shape, lambda b, c, pt: (b, 0, 0))
    grid_spec = pltpu.PrefetchScalarGridSpec(
        num_scalar_prefetch=1,
        grid=(n, n_pages // gp),
        in_specs=[per_seq((IDX_HEADS, IDX_DIM)), per_seq((IDX_HEADS, 1)), per_seq((1, IDX_DIM))]
        + [page_spec(g) for g in range(gp)],
        out_specs=[pl.BlockSpec((1, gp, PAGE_SIZE), lambda b, c, pt: (b, c, 0)), per_seq((1, PAGE_SIZE))],
    )
    return pl.pallas_call(
        functools.partial(_sample_scores_kernel, gp=gp),
        grid_spec=grid_spec,
        out_shape=[jax.ShapeDtypeStruct((n, n_pages, PAGE_SIZE), I32), jax.ShapeDtypeStruct((n, 1, PAGE_SIZE), I32)],
        compiler_params=_params(("parallel", "arbitrary")),
        name="sample_scores",
    )(page_table, qi, wi, ki_new, *([cache_idx_t] * gp))


def _sample_select_kernel(keys_ref, newkey_ref, madd_o, newm_o, *, k_sel):
    n, past = keys_ref.shape
    key_new = newkey_ref[:, :1]
    idx = lax.broadcasted_iota(I32, (n, past), 1)

    def total(ind_past, ind_new):
        return jnp.sum(ind_past.astype(I32), axis=1, keepdims=True) + ind_new.astype(I32)

    lo0 = jnp.full((n, 1), KEY_NEG_INF + 1, I32)
    hi0 = jnp.full((n, 1), KEY_MAX, I32)
    thr, c_thr, c_above = _bisect_kth(lambda mid: total(keys_ref[...] >= mid, key_new >= mid),
                                      lo0, hi0, jnp.full((n, 1), past + 1, I32), jnp.zeros((n, 1), I32), k_sel)
    has_tie = c_thr > k_sel
    need = k_sel - c_above
    tie_j = _bisect_tie_index(
        lambda mid: total(jnp.logical_and(keys_ref[...] == thr, idx <= mid),
                          jnp.logical_and(key_new == thr, past <= mid)),
        need, jnp.max(has_tie.astype(I32)), past, 14)
    tie_j = jnp.where(has_tie, tie_j, KEY_MAX)
    t = keys_ref[...]
    tie_ok = jnp.where(idx <= tie_j, 0.0, NEG_BIG)
    madd_o[...] = jnp.where(t > thr, 0.0, jnp.where(t == thr, tie_ok, NEG_BIG)).astype(F32)
    new_ok = jnp.where(past <= tie_j, 0.0, NEG_BIG)
    newm = jnp.where(key_new > thr, 0.0, jnp.where(key_new == thr, new_ok, NEG_BIG)).astype(F32)
    newm_o[...] = jnp.broadcast_to(newm, (n, PAGE_SIZE))


def _sample_select(keys, newkey):
    n, past = keys.shape
    k_sel = min(TOPK_MAX, (past + 1) // 4)
    return pl.pallas_call(
        functools.partial(_sample_select_kernel, k_sel=k_sel),
        grid=(1,),
        in_specs=[_const_spec(keys.shape), _const_spec(newkey.shape)],
        out_specs=[_const_spec(keys.shape), _const_spec(newkey.shape)],
        out_shape=[jax.ShapeDtypeStruct(keys.shape, F32), jax.ShapeDtypeStruct(newkey.shape, F32)],
        compiler_params=_params(("arbitrary",)),
        name="sample_select",
    )(keys, newkey)


def _attn_sample_kernel(pt_ref, q_ref, kn_ref, vn_ref, madd_ref, newm_ref, bias_ref, biasn_ref, *rest, n_pages, gp):
    k_pages = rest[:gp]
    v_pages = rest[gp:2 * gp]
    yb_o, m_scr, l_scr, acc_scr = rest[2 * gp:]
    c = pl.program_id(1)
    n_chunks = n_pages // gp
    head_of_lane = lax.broadcasted_iota(I32, (N_HEADS, ATT_WIDTH), 1) // HEAD_DIM
    head_of_row = lax.broadcasted_iota(I32, (N_HEADS, ATT_WIDTH), 0)
    own = head_of_lane == head_of_row
    q_blk = jnp.where(own, jnp.broadcast_to(q_ref[0], (N_HEADS, ATT_WIDTH)), 0.0)

    @pl.when(c == 0)
    def _():
        kn = kn_ref[0].astype(BF16).astype(F32)
        x_new = (jnp.sum(q_blk.astype(BF16).astype(F32) * kn, axis=1, keepdims=True)
                 + biasn_ref[...] + newm_ref[0][:, :1])
        m_scr[...] = x_new
        l_scr[...] = jnp.ones_like(l_scr)
        acc_scr[...] = jnp.broadcast_to(vn_ref[0].astype(BF16).astype(F32), (N_HEADS, ATT_WIDTH))

    q_bf = q_blk.astype(BF16)
    xs = []
    for g in range(gp):
        pg = c * gp + g
        kt = k_pages[g][0].astype(BF16)
        xs.append(_dot(q_bf, kt) + madd_ref[0, pg] + bias_ref[pg])
    x = jnp.concatenate(xs, axis=1)
    m = m_scr[...]
    m_new = jnp.maximum(m, jnp.max(x, axis=1, keepdims=True))
    alpha = jnp.exp2(m - m_new)
    p = jnp.exp2(x - m_new)
    l_scr[...] = alpha * l_scr[...] + jnp.sum(p, axis=1, keepdims=True)
    p_bf = p.astype(BF16)
    pv = jnp.zeros((N_HEADS, ATT_WIDTH), F32)
    for g in range(gp):
        vt = v_pages[g][0].astype(BF16)
        pv = pv + _dot_nt(p_bf[:, g * PAGE_SIZE:(g + 1) * PAGE_SIZE], vt)
    acc_scr[...] = alpha * acc_scr[...] + pv
    m_scr[...] = m_new

    @pl.when(c == n_chunks - 1)
    def _():
        o = acc_scr[...] / l_scr[...]
        yb_o[0] = jnp.sum(jnp.where(own, o, 0.0), axis=0, keepdims=True).astype(BF16)


def _attn_sample(page_table, q, k_new, v_new, madd, newm, bias_pages, bias_new, cache_kt, cache_vt):
    n, n_pages = page_table.shape
    gp = min(SAMPLE_KV_PAGES, n_pages)
    assert n_pages % gp == 0
    page_spec = lambda g: pl.BlockSpec((1, ATT_WIDTH, PAGE_SIZE), lambda b, c, pt: (pt[b, c * gp + g], 0, 0))
    per_seq = lambda shape: pl.BlockSpec((1,) + shape, lambda b, c, pt: (b, 0, 0))
    grid_spec = pltpu.PrefetchScalarGridSpec(
        num_scalar_prefetch=1,
        grid=(n, n_pages // gp),
        in_specs=[per_seq((1, ATT_WIDTH)), per_seq((1, ATT_WIDTH)), per_seq((1, ATT_WIDTH)),
                  pl.BlockSpec((1, n_pages, 1, PAGE_SIZE), lambda b, c, pt: (b, 0, 0, 0)), per_seq((1, PAGE_SIZE)),
                  pl.BlockSpec(bias_pages.shape, lambda b, c, pt: (0, 0, 0)),
                  pl.BlockSpec(bias_new.shape, lambda b, c, pt: (0, 0))]
        + [page_spec(g) for g in range(gp)] * 2,
        out_specs=per_seq((1, ATT_WIDTH)),
        scratch_shapes=[pltpu.VMEM((N_HEADS, 1), F32), pltpu.VMEM((N_HEADS, 1), F32),
                        pltpu.VMEM((N_HEADS, ATT_WIDTH), F32)],
    )
    return pl.pallas_call(
        functools.partial(_attn_sample_kernel, n_pages=n_pages, gp=gp),
        grid_spec=grid_spec,
        out_shape=jax.ShapeDtypeStruct((n, 1, ATT_WIDTH), BF16),
        compiler_params=_params(("parallel", "arbitrary")),
        name="attn_sample",
    )(page_table, q, k_new, v_new, madd, newm, bias_pages, bias_new, *([cache_kt] * gp), *([cache_vt] * gp))


def _merge_kernel(x_ref, ya_ref, yb_ref, sga_ref, sgb_ref, wa, wb, wo, g_ref, b_ref, x1_o, *, alpha):
    merged = (sga_ref[...].astype(F32) * _dot(ya_ref[...], wa[...])
              + sgb_ref[...].astype(F32) * _dot(yb_ref[...], wb[...]))
    z = alpha * x_ref[...] + _dot(merged.astype(BF16), wo[...])
    x1_o[...] = _layer_norm(z, g_ref[...], b_ref[...])


def _merge(x, ya, yb, sga, sgb, wa, wb, wo, g, b, alpha):
    n = x.shape[0]
    tm = min(TOKEN_TILE, n)
    rows = lambda c: pl.BlockSpec((tm, c), lambda i: (i, 0))
    return pl.pallas_call(
        functools.partial(_merge_kernel, alpha=alpha),
        grid=(n // tm,),
        in_specs=[rows(D_MODEL), rows(SSM_WIDTH), rows(ATT_WIDTH), rows(D_MODEL), rows(D_MODEL),
                  _const_spec(wa.shape), _const_spec(wb.shape), _const_spec(wo.shape),
                  _const_spec(g.shape), _const_spec(b.shape)],
        out_specs=rows(D_MODEL),
        out_shape=jax.ShapeDtypeStruct((n, D_MODEL), F32),
        compiler_params=_params(("parallel",)),
        name="merge_out_proj",
    )(x, ya, yb, sga, sgb, wa, wb, wo, g, b)


def _ffn_kernel(x1_ref, wg, wu, wd, g_ref, b_ref, y_o, acc_scr, *, alpha):
    x1 = x1_ref[...]
    xb = x1.astype(BF16)
    acc_scr[...] = alpha * x1
    for c in range(D_FF // FF_CHUNK):
        sl = slice(c * FF_CHUNK, (c + 1) * FF_CHUNK)
        gate = _dot(xb, wg[:, sl])
        up = _dot(xb, wu[:, sl])
        hid = (jax.nn.silu(gate) * up).astype(BF16)
        acc_scr[...] += _dot(hid, wd[sl, :])
    y_o[...] = _layer_norm(acc_scr[...], g_ref[...], b_ref[...])


def _ffn(x1, wg, wu, wd, g, b, alpha):
    n = x1.shape[0]
    tm = min(TOKEN_TILE, n)
    rows = pl.BlockSpec((tm, D_MODEL), lambda i: (i, 0))
    return pl.pallas_call(
        functools.partial(_ffn_kernel, alpha=alpha),
        grid=(n // tm,),
        in_specs=[rows, _const_spec(wg.shape), _const_spec(wu.shape), _const_spec(wd.shape),
                  _const_spec(g.shape), _const_spec(b.shape)],
        out_specs=rows,
        out_shape=jax.ShapeDtypeStruct((n, D_MODEL), F32),
        scratch_shapes=[pltpu.VMEM((tm, D_MODEL), F32)],
        compiler_params=_params(("parallel",)),
        name="ffn",
    )(x1, wg, wu, wd, g, b)


def _rel_bucket(rel):
    n = jnp.maximum(rel, 0)
    max_exact = REL_BUCKETS // 2
    nf = jnp.maximum(n, max_exact).astype(F32)
    large = max_exact + (jnp.log(nf / max_exact) / math.log(REL_MAX_DIST / max_exact)
                         * (REL_BUCKETS - max_exact)).astype(I32)
    large = jnp.minimum(large, REL_BUCKETS - 1)
    return jnp.where(n < max_exact, n, large)


def _bias_lookup(rel_bias, rel):
    onehot = (_rel_bucket(rel)[..., None] == jnp.arange(REL_BUCKETS, dtype=I32)).astype(F32)
    table = rel_bias.astype(F32) * LOG2E
    return jnp.einsum('...k,kh->...h', onehot, table, precision=lax.Precision.HIGHEST)


def _prompt_bias_tiles(rel_bias):
    kl = jnp.arange(KV_TILE, dtype=I32)[:, None]
    ql = jnp.arange(Q_TILE, dtype=I32)[None, :]
    rel = jnp.stack([ql - kl, ql - kl + KV_TILE])
    assert KV_TILE + 1 >= REL_MAX_DIST
    near = jnp.moveaxis(_bias_lookup(rel_bias, rel), -1, 0)
    far = _bias_lookup(rel_bias, jnp.full((), 2 * KV_TILE, I32))
    return near, far


def _ssm_operators(a_re, a_im, log_dt, b_re, b_im, c_re, c_im, d):
    g, p, gi = SSM_GROUPS, SSM_STATE, SSM_GROUP
    a_re, a_im = a_re.astype(F32), a_im.astype(F32)
    dt = jnp.exp(log_dt.astype(F32))[:, None]
    mag = jnp.exp(dt * a_re)
    lre = mag * jnp.cos(dt * a_im)
    lim = mag * jnp.sin(dt * a_im)
    den = a_re * a_re + a_im * a_im
    fr = ((lre - 1.0) * a_re + lim * a_im) / den
    fi = (lim * a_re - (lre - 1.0) * a_im) / den
    bbr = fr[:, :, None] * b_re - fi[:, :, None] * b_im
    bbi = fr[:, :, None] * b_im + fi[:, :, None] * b_re
    eye = jnp.eye(g, dtype=F32)
    blockdiag_in = lambda m: jnp.einsum('gpi,gh->gihp', m, eye).reshape(g * gi, g * p)
    blockdiag_out = lambda m: jnp.einsum('gip,gh->gphi', m, eye).reshape(g * p, g * gi)
    bfull = jnp.concatenate([blockdiag_in(bbr), blockdiag_in(bbi)], axis=1)
    cfull = jnp.concatenate([blockdiag_out(c_re.astype(F32)), blockdiag_out(-c_im.astype(F32))], axis=0)
    sc = SLAB_COLS
    bslab = jnp.stack([jnp.concatenate([bfull[s * 128:(s + 1) * 128, s * sc:(s + 1) * sc],
                                        bfull[s * 128:(s + 1) * 128, g * p + s * sc:g * p + (s + 1) * sc]], axis=1)
                       for s in range(SSM_SLABS)])
    cslab = jnp.stack([jnp.concatenate([cfull[s * sc:(s + 1) * sc, s * 128:(s + 1) * 128],
                                        cfull[g * p + s * sc:g * p + (s + 1) * sc, s * 128:(s + 1) * 128]], axis=0)
                       for s in range(SSM_SLABS)])
    return {'lre': lre.reshape(1, g * p), 'lim': lim.reshape(1, g * p),
            'bfull': bfull.astype(BF16), 'cfull': cfull.astype(BF16),
            'bslab': bslab.astype(BF16), 'cslab': cslab.astype(BF16), 'd': d.astype(F32).reshape(1, g * gi)}


def _split_w_in(w_in):
    offs = [0]
    for n in IN_SPLITS:
        offs.append(offs[-1] + n)
    names = ['u', 'q', 'k', 'v', 'qi', 'ki', 'wi', 'ga', 'gb']
    return {nm: w_in[:, offs[j]:offs[j + 1]] for j, nm in enumerate(names)}


def kernel(x_prompt, x_sample, cache_k, cache_v, cache_idx_k, state_ssm_re, state_ssm_im, page_table, w_in, ssm_a_re, ssm_a_im, ssm_log_dt, ssm_b_re, ssm_b_im, ssm_c_re, ssm_c_im, ssm_d, w_glu, w_branch_a, w_branch_b, rel_bias, w_out, ln1_g, ln1_b, w_ffn_in, w_ffn_out, ln2_g, ln2_b):
    depth = w_in.shape[0]
    bsz, s_len, _ = x_prompt.shape
    n_dec, dec_seq, _ = x_sample.shape
    assert dec_seq == 1
    n_pages = page_table.shape[1]
    n_pool = cache_k.shape[1]
    past = n_pages * PAGE_SIZE
    alpha = (2 * depth) ** 0.25
    idx_scale = IDX_DIM ** -0.5 * IDX_HEADS ** -0.5
    att_scale = HEAD_DIM ** -0.5 * LOG2E

    bias_t, bias_far = _prompt_bias_tiles(rel_bias)
    key_pos = jnp.arange(past, dtype=I32).reshape(n_pages, PAGE_SIZE)
    bias_pages = jnp.swapaxes(_bias_lookup(rel_bias, past - key_pos), 1, 2)
    bias_new = _bias_lookup(rel_bias, jnp.zeros((), I32)).reshape(N_HEADS, 1)

    x_p = x_prompt
    x_s = x_sample.reshape(n_dec, D_MODEL)
    outs = {k: [] for k in ('kp', 'vp', 'kip', 'hpr', 'hpi', 'ks', 'vs', 'kis', 'hsr', 'hsi')}
    for l in range(depth):
        w = _split_w_in(w_in[l])
        wn = {nm: w[nm].astype(BF16) for nm in ('u', 'k', 'v', 'ki', 'ga', 'gb')}
        wn['tq'] = (w['q'] * att_scale).T.astype(BF16)
        wn['tv'] = w['v'].T.astype(BF16)
        wn['tqi'] = w['qi'].T.astype(BF16)
        wn['twi'] = jnp.pad((w['wi'] * idx_scale).T, ((0, 16 - IDX_HEADS), (0, 0))).astype(BF16)
        ws = {nm: w[nm].astype(BF16) for nm in ('u', 'k', 'v', 'qi', 'ki', 'ga', 'gb')}
        ws['q'] = (w['q'] * att_scale).astype(BF16)
        ws['wi'] = (w['wi'] * idx_scale).astype(BF16)
        ops = _ssm_operators(ssm_a_re[l], ssm_a_im[l], ssm_log_dt[l], ssm_b_re[l], ssm_b_im[l],
                             ssm_c_re[l], ssm_c_im[l], ssm_d[l])
        wglu = w_glu[l].astype(BF16)
        wa, wb, wo = w_branch_a[l].astype(BF16), w_branch_b[l].astype(BF16), w_out[l].astype(BF16)
        wg, wu = w_ffn_in[l][:, :D_FF].astype(BF16), w_ffn_in[l][:, D_FF:].astype(BF16)
        wd = w_ffn_out[l].astype(BF16)
        g1, b1 = ln1_g[l].astype(F32).reshape(1, D_MODEL), ln1_b[l].astype(F32).reshape(1, D_MODEL)
        g2, b2 = ln2_g[l].astype(F32).reshape(1, D_MODEL), ln2_b[l].astype(F32).reshape(1, D_MODEL)

        u, k, v, ki, k_bf, ki_bf, sga, sgb, qt, vt, qit, wit = _in_proj_prompt(x_p, wn)
        ya, hpr, hpi = _ssm_prompt(u, ops, wglu)
        yb = _attn_prompt(qt, qit, wit, k_bf, vt, ki_bf, bias_t, bias_far)
        n_tok = bsz * s_len
        flat = lambda a: a.reshape(n_tok, a.shape[-1])
        x1 = _merge(flat(x_p), flat(ya), flat(yb), flat(sga), flat(sgb), wa, wb, wo, g1, b1, alpha)
        x_p = _ffn(x1, wg, wu, wd, g2, b2, alpha).reshape(bsz, s_len, D_MODEL)
        outs['kp'].append(k.reshape(bsz, s_len, N_HEADS, HEAD_DIM))
        outs['vp'].append(v.reshape(bsz, s_len, N_HEADS, HEAD_DIM))
        outs['kip'].append(ki)
        outs['hpr'].append(hpr.reshape(bsz, SSM_GROUPS, SSM_STATE))
        outs['hpi'].append(hpi.reshape(bsz, SSM_GROUPS, SSM_STATE))

        us, qs, k_s, v_s, qis, kis, wis, sga_s, sgb_s = _in_proj_sample(x_s, ws)
        ya_s, hsr, hsi = _ssm_sample(us, state_ssm_re[l].reshape(n_dec, SSM_COLS).astype(F32),
                                     state_ssm_im[l].reshape(n_dec, SSM_COLS).astype(F32), ops, wglu)
        cache_kt = jnp.transpose(cache_k[l], (0, 2, 3, 1)).reshape(n_pool, ATT_WIDTH, PAGE_SIZE)
        cache_vt = jnp.transpose(cache_v[l], (0, 2, 3, 1)).reshape(n_pool, ATT_WIDTH, PAGE_SIZE)
        cache_it = jnp.transpose(cache_idx_k[l], (0, 2, 1))
        keys, newkey = _sample_scores(
            page_table, qis.astype(BF16).reshape(n_dec, IDX_HEADS, IDX_DIM), wis.reshape(n_dec, IDX_HEADS, 1),
            kis.astype(BF16).reshape(n_dec, 1, IDX_DIM), cache_it)
        madd, newm = _sample_select(keys.reshape(n_dec, past), newkey.reshape(n_dec, PAGE_SIZE))
        yb_s = _attn_sample(page_table, qs.reshape(n_dec, 1, ATT_WIDTH), k_s.reshape(n_dec, 1, ATT_WIDTH),
                            v_s.reshape(n_dec, 1, ATT_WIDTH), madd.reshape(n_dec, n_pages, 1, PAGE_SIZE),
                            newm.reshape(n_dec, 1, PAGE_SIZE), bias_pages, bias_new, cache_kt, cache_vt)
        x1_s = _merge(x_s, ya_s, yb_s.reshape(n_dec, ATT_WIDTH), sga_s, sgb_s, wa, wb, wo, g1, b1, alpha)
        x_s = _ffn(x1_s, wg, wu, wd, g2, b2, alpha)
        outs['ks'].append(k_s.reshape(n_dec, 1, N_HEADS, HEAD_DIM))
        outs['vs'].append(v_s.reshape(n_dec, 1, N_HEADS, HEAD_DIM))
        outs['kis'].append(kis.reshape(n_dec, 1, IDX_DIM))
        outs['hsr'].append(hsr.reshape(n_dec, SSM_GROUPS, SSM_STATE))
        outs['hsi'].append(hsi.reshape(n_dec, SSM_GROUPS, SSM_STATE))

    st = lambda name: jnp.stack(outs[name])
    return (x_p, x_s.reshape(n_dec, 1, D_MODEL),
            st('kp'), st('vp'), st('kip'), st('hpr'), st('hpi'),
            st('ks'), st('vs'), st('kis'), st('hsr'), st('hsi'))
```

```python
import functools
import math

import jax
import jax.numpy as jnp
from jax import lax
from jax.experimental import pallas as pl
from jax.experimental.pallas import tpu as pltpu

F32 = jnp.float32
BF16 = jnp.bfloat16
I32 = jnp.int32

D_MODEL = 1024
SSM_WIDTH = 512
SSM_GROUP = 16
SSM_GROUPS = 32
SSM_STATE = 64
SSM_COLS = SSM_GROUPS * SSM_STATE
N_HEADS = 8
HEAD_DIM = 64
ATT_WIDTH = 512
IDX_HEADS = 8
IDX_DIM = 32
TOPK_MAX = 256
PAGE_SIZE = 128
REL_BUCKETS = 32
REL_MAX_DIST = 128
D_FF = 2816
LN_EPS = 1e-5
IN_SPLITS = (SSM_WIDTH, ATT_WIDTH, ATT_WIDTH, ATT_WIDTH, IDX_HEADS * IDX_DIM, IDX_DIM, IDX_HEADS, D_MODEL, D_MODEL)
LOG2E = math.log2(math.e)

VMEM_LIMIT_BYTES = 56 * 1024 * 1024
NEG_BIG = -1e30
KEY_NEG_INF = -2139095041
KEY_MAX = 2147483647

TOKEN_TILE = 512
Q_TILE = 256
KV_TILE = 256
SSM_T_TILE = 64
FF_CHUNK = 256
SAMPLE_IDX_PAGES = 64
SAMPLE_KV_PAGES = 32


def _params(sem):
    return pltpu.CompilerParams(dimension_semantics=sem, vmem_limit_bytes=VMEM_LIMIT_BYTES)


def _const_spec(shape):
    n = len(shape)
    return pl.BlockSpec(shape, lambda *_: (0,) * n)


def _resident_spec(block, index_map):
    return pl.BlockSpec(block, index_map, pipeline_mode=pl.Buffered(1))


def _dot(a, b):
    return jnp.dot(a, b, preferred_element_type=F32)


def _dot_nt(a, b):
    return lax.dot_general(a, b, (((1,), (1,)), ((), ())), preferred_element_type=F32)


def _layer_norm(z, g, b):
    mu = jnp.mean(z, axis=-1, keepdims=True)
    zc = z - mu
    var = jnp.mean(zc * zc, axis=-1, keepdims=True)
    return zc * lax.rsqrt(var + LN_EPS) * g + b


def _ordered_key(x):
    b = pltpu.bitcast(x, I32)
    return b ^ ((b >> 31) & KEY_MAX)


def _in_proj_prompt_kernel(x_ref, wu, wk, wv, wga, wgb, wki, wtq, wtv, wtqi, wtwi,
                           u_o, k_o, v_o, ki_o, kbf_o, kibf_o, sga_o, sgb_o, qt_o, vt_o, qit_o, wit_o):
    xb = x_ref[0].astype(BF16)
    u_o[0] = _dot(xb, wu[...])
    k = _dot(xb, wk[...])
    k_o[0] = k
    kbf_o[0] = k.astype(BF16)
    v_o[0] = _dot(xb, wv[...])
    ki = _dot(xb, wki[...])
    ki_o[0] = ki
    kibf_o[0] = ki.astype(BF16)
    sga_o[0] = jax.nn.sigmoid(_dot(xb, wga[...])).astype(BF16)
    sgb_o[0] = jax.nn.sigmoid(_dot(xb, wgb[...])).astype(BF16)
    qt_o[0] = _dot_nt(wtq[...], xb).astype(BF16)
    vt_o[0] = _dot_nt(wtv[...], xb).astype(BF16)
    qit_o[0] = _dot_nt(wtqi[...], xb).astype(BF16)
    wit_o[0] = _dot_nt(wtwi[...], xb)[:IDX_HEADS]


def _in_proj_prompt(x, w):
    bsz, s_len, _ = x.shape
    tm = min(TOKEN_TILE, s_len)
    nat = lambda c: pl.BlockSpec((1, tm, c), lambda b, i: (b, i, 0))
    tr = lambda c: pl.BlockSpec((1, c, tm), lambda b, i: (b, 0, i))
    sds = jax.ShapeDtypeStruct
    weights = [w['u'], w['k'], w['v'], w['ga'], w['gb'], w['ki'], w['tq'], w['tv'], w['tqi'], w['twi']]
    return pl.pallas_call(
        _in_proj_prompt_kernel,
        grid=(bsz, s_len // tm),
        in_specs=[nat(D_MODEL)] + [_const_spec(a.shape) for a in weights],
        out_specs=[nat(SSM_WIDTH), nat(ATT_WIDTH), nat(ATT_WIDTH), nat(IDX_DIM), nat(ATT_WIDTH), nat(IDX_DIM),
                   nat(D_MODEL), nat(D_MODEL), tr(ATT_WIDTH), tr(ATT_WIDTH), tr(IDX_HEADS * IDX_DIM), tr(IDX_HEADS)],
        out_shape=[sds((bsz, s_len, SSM_WIDTH), F32), sds((bsz, s_len, ATT_WIDTH), F32),
                   sds((bsz, s_len, ATT_WIDTH), F32), sds((bsz, s_len, IDX_DIM), F32),
                   sds((bsz, s_len, ATT_WIDTH), BF16), sds((bsz, s_len, IDX_DIM), BF16),
                   sds((bsz, s_len, D_MODEL), BF16), sds((bsz, s_len, D_MODEL), BF16),
                   sds((bsz, ATT_WIDTH, s_len), BF16), sds((bsz, ATT_WIDTH, s_len), BF16),
                   sds((bsz, IDX_HEADS * IDX_DIM, s_len), BF16), sds((bsz, IDX_HEADS, s_len), F32)],
        compiler_params=_params(("parallel", "parallel")),
        name="in_proj_prompt",
    )(x, *weights)


def _in_proj_sample_kernel(x_ref, wu, wq, wk, wv, wqi, wki, wwi, wga, wgb,
                           u_o, q_o, k_o, v_o, qi_o, ki_o, wi_o, sga_o, sgb_o):
    xb = x_ref[...].astype(BF16)
    u_o[...] = _dot(xb, wu[...])
    q_o[...] = _dot(xb, wq[...])
    k_o[...] = _dot(xb, wk[...])
    v_o[...] = _dot(xb, wv[...])
    qi_o[...] = _dot(xb, wqi[...])
    ki_o[...] = _dot(xb, wki[...])
    wi_o[...] = _dot(xb, wwi[...])
    sga_o[...] = jax.nn.sigmoid(_dot(xb, wga[...])).astype(BF16)
    sgb_o[...] = jax.nn.sigmoid(_dot(xb, wgb[...])).astype(BF16)


def _in_proj_sample(x, w):
    n = x.shape[0]
    weights = [w['u'], w['q'], w['k'], w['v'], w['qi'], w['ki'], w['wi'], w['ga'], w['gb']]
    cols = [SSM_WIDTH, ATT_WIDTH, ATT_WIDTH, ATT_WIDTH, IDX_HEADS * IDX_DIM, IDX_DIM, IDX_HEADS, D_MODEL, D_MODEL]
    dts = [F32] * 7 + [BF16, BF16]
    return pl.pallas_call(
        _in_proj_sample_kernel,
        grid=(1,),
        in_specs=[_const_spec(x.shape)] + [_const_spec(a.shape) for a in weights],
        out_specs=[_const_spec((n, c)) for c in cols],
        out_shape=[jax.ShapeDtypeStruct((n, c), d) for c, d in zip(cols, dts)],
        compiler_params=_params(("arbitrary",)),
        name="in_proj_sample",
    )(x, *weights)


def _glu_gate(y, wglu):
    y = jax.nn.gelu(y)
    return y * jax.nn.sigmoid(_dot(y.astype(BF16), wglu[...]))


def _glu_tail(h_bf, u, cfull, dvec, wglu):
    return _glu_gate(_dot(h_bf, cfull[...]) + dvec[...] * u, wglu)


SSM_SLABS = SSM_WIDTH // 128
SLAB_COLS = SSM_COLS // SSM_SLABS


def _ssm_prompt_kernel(u_ref, perm_ref, permt_ref, bslab, cslab, lre_ref, lim_ref, dvec, wglu,
                       ya_o, hre_o, him_o, h_scr, u_scr, y_scr, state_scr, *, bsz, tt):
    c = pl.program_id(0)
    rows = bsz * tt
    row_chunk = 256

    @pl.when(c == 0)
    def _():
        state_scr[...] = jnp.zeros_like(state_scr)

    u_bt = jnp.concatenate([u_ref[b] for b in range(bsz)], axis=0)
    u_hi = u_bt.astype(BF16)
    u_lo = (u_bt - u_hi.astype(F32)).astype(BF16)
    u_scr[...] = _dot(perm_ref[...], u_hi) + _dot(perm_ref[...], u_lo)
    for r in range(rows // row_chunk):
        sl = slice(r * row_chunk, (r + 1) * row_chunk)
        u_bf = u_scr[sl, :].astype(BF16)
        for s in range(SSM_SLABS):
            bu = _dot(u_bf[:, s * 128:(s + 1) * 128], bslab[s])
            h_scr[sl, s * SLAB_COLS:(s + 1) * SLAB_COLS] = bu[:, :SLAB_COLS]
            h_scr[sl, SSM_COLS + s * SLAB_COLS:SSM_COLS + (s + 1) * SLAB_COLS] = bu[:, SLAB_COLS:]

    chunk = 512
    for ci in range(SSM_COLS // chunk):
        re_sl = slice(ci * chunk, (ci + 1) * chunk)
        im_sl = slice(SSM_COLS + ci * chunk, SSM_COLS + (ci + 1) * chunk)
        lre = lre_ref[:, re_sl]
        lim = lim_ref[:, re_sl]

        def step(t, carry):
            hr, hi = carry
            r0 = pl.multiple_of(t * bsz, bsz)
            nr = lre * hr - lim * hi + h_scr[pl.ds(r0, bsz), re_sl]
            ni = lre * hi + lim * hr + h_scr[pl.ds(r0, bsz), im_sl]
            h_scr[pl.ds(r0, bsz), re_sl] = nr
            h_scr[pl.ds(r0, bsz), im_sl] = ni
            return nr, ni

        hr, hi = lax.fori_loop(0, tt, step, (state_scr[:, re_sl], state_scr[:, im_sl]), unroll=4)
        state_scr[:, re_sl] = hr
        state_scr[:, im_sl] = hi

    for r in range(rows // row_chunk):
        sl = slice(r * row_chunk, (r + 1) * row_chunk)
        ys = []
        for s in range(SSM_SLABS):
            h_re = h_scr[sl, s * SLAB_COLS:(s + 1) * SLAB_COLS].astype(BF16)
            h_im = h_scr[sl, SSM_COLS + s * SLAB_COLS:SSM_COLS + (s + 1) * SLAB_COLS].astype(BF16)
            ys.append(_dot(h_re, cslab[s, :SLAB_COLS]) + _dot(h_im, cslab[s, SLAB_COLS:]))
        y = jnp.concatenate(ys, axis=1) + dvec[...] * u_scr[sl, :]
        y_scr[sl, :] = _glu_gate(y, wglu).astype(BF16)

    ya_bt = _dot(permt_ref[...], y_scr[...]).astype(BF16)
    for b in range(bsz):
        ya_o[b] = ya_bt[b * tt:(b + 1) * tt]

    @pl.when(c == pl.num_programs(0) - 1)
    def _():
        hre_o[...] = state_scr[:, :SSM_COLS]
        him_o[...] = state_scr[:, SSM_COLS:]


def _ssm_prompt(u, ops, wglu):
    bsz, s_len, _ = u.shape
    tt = min(SSM_T_TILE, s_len)
    rows = bsz * tt
    lre = jnp.broadcast_to(ops['lre'], (bsz, SSM_COLS))
    lim = jnp.broadcast_to(ops['lim'], (bsz, SSM_COLS))
    kern = functools.partial(_ssm_prompt_kernel, bsz=bsz, tt=tt)
    src = (jnp.arange(rows, dtype=I32) % bsz) * tt + jnp.arange(rows, dtype=I32) // bsz
    perm = (src[:, None] == jnp.arange(rows, dtype=I32)[None, :]).astype(BF16)
    return pl.pallas_call(
        kern,
        grid=(s_len // tt,),
        in_specs=[pl.BlockSpec((bsz, tt, SSM_WIDTH), lambda c: (0, c, 0)),
                  _const_spec(perm.shape), _const_spec(perm.shape),
                  _const_spec(ops['bslab'].shape), _const_spec(ops['cslab'].shape),
                  _const_spec(lre.shape), _const_spec(lim.shape), _const_spec(ops['d'].shape),
                  _const_spec(wglu.shape)],
        out_specs=[pl.BlockSpec((bsz, tt, SSM_WIDTH), lambda c: (0, c, 0)),
                   _const_spec((bsz, SSM_COLS)), _const_spec((bsz, SSM_COLS))],
        out_shape=[jax.ShapeDtypeStruct((bsz, s_len, SSM_WIDTH), BF16),
                   jax.ShapeDtypeStruct((bsz, SSM_COLS), F32), jax.ShapeDtypeStruct((bsz, SSM_COLS), F32)],
        scratch_shapes=[pltpu.VMEM((rows, 2 * SSM_COLS), F32), pltpu.VMEM((rows, SSM_WIDTH), F32),
                        pltpu.VMEM((rows, SSM_WIDTH), BF16), pltpu.VMEM((bsz, 2 * SSM_COLS), F32)],
        compiler_params=_params(("arbitrary",)),
        name="ssm_prompt",
    )(u, perm, perm.T, ops['bslab'], ops['cslab'], lre, lim, ops['d'], wglu)


def _ssm_sample_kernel(u_ref, h0re_ref, h0im_ref, bfull, cfull, lre_ref, lim_ref, dvec, wglu,
                       ya_o, hre_o, him_o):
    u = u_ref[...]
    bu = _dot(u.astype(BF16), bfull[...])
    lre = lre_ref[...]
    lim = lim_ref[...]
    h0r = h0re_ref[...]
    h0i = h0im_ref[...]
    hr = lre * h0r - lim * h0i + bu[:, :SSM_COLS]
    hi = lre * h0i + lim * h0r + bu[:, SSM_COLS:]
    hre_o[...] = hr
    him_o[...] = hi
    h_bf = jnp.concatenate([hr.astype(BF16), hi.astype(BF16)], axis=1)
    ya_o[...] = _glu_tail(h_bf, u, cfull, dvec, wglu).astype(BF16)


def _ssm_sample(u, h0re, h0im, ops, wglu):
    n = u.shape[0]
    args = [u, h0re, h0im, ops['bfull'], ops['cfull'], ops['lre'], ops['lim'], ops['d'], wglu]
    return pl.pallas_call(
        _ssm_sample_kernel,
        grid=(1,),
        in_specs=[_const_spec(a.shape) for a in args],
        out_specs=[_const_spec((n, SSM_WIDTH)), _const_spec((n, SSM_COLS)), _const_spec((n, SSM_COLS))],
        out_shape=[jax.ShapeDtypeStruct((n, SSM_WIDTH), BF16),
                   jax.ShapeDtypeStruct((n, SSM_COLS), F32), jax.ShapeDtypeStruct((n, SSM_COLS), F32)],
        compiler_params=_params(("arbitrary",)),
        name="ssm_sample",
    )(*args)


def _bisect_step(count_ge, k, st):
    lo, hi, c_lo, c_hi = st
    mid = (lo >> 1) + (hi >> 1) + (lo & hi & 1)
    cnt = count_ge(mid)
    active = lo + 1 != hi
    up = jnp.logical_and(active, cnt >= k)
    down = jnp.logical_and(active, cnt < k)
    n_lo = jnp.where(up, mid, lo)
    n_clo = jnp.where(up, cnt, c_lo)
    n_hi = jnp.where(down, mid, jnp.where(jnp.logical_and(active, cnt == k), mid + 1, hi))
    n_chi = jnp.where(down, cnt, c_hi)
    return n_lo, n_hi, n_clo, n_chi


def _bisect_kth(count_ge, lo, hi, c_lo, c_hi, k, steps_per_check=1):
    def cond(st):
        it, lo, hi, _, _ = st
        return jnp.logical_and(it < 34, jnp.max((lo + 1 != hi).astype(I32)) > 0)

    def body(st):
        it, st = st[0], st[1:]
        for _ in range(steps_per_check):
            st = _bisect_step(count_ge, k, st)
        return (it + steps_per_check,) + st

    _, lo, hi, c_lo, c_hi = lax.while_loop(cond, body, (jnp.int32(0), lo, hi, c_lo, c_hi))
    return lo, c_lo, c_hi


def _bisect_kth_fixed(count_ge, lo, hi, c_lo, c_hi, k):
    width = hi - lo
    n_steps = jnp.max(32 - lax.clz(width - 1))
    lo, hi, c_lo, c_hi = lax.fori_loop(0, n_steps, lambda _, st: _bisect_step(count_ge, k, st),
                                       (lo, hi, c_lo, c_hi))
    return lo, c_lo, c_hi


def _bisect_tie_index(count_tie_le, need, n_tie_rows, idx_hi, n_bits):
    def cond(st):
        it, _, _ = st
        return jnp.logical_and(it < n_bits, n_tie_rows > 0)

    def body(st):
        it, lo, hi = st
        mid = (lo + hi) >> 1
        ok = count_tie_le(mid) >= need
        return it + 1, jnp.where(ok, lo, mid), jnp.where(ok, mid, hi)

    lo0 = jnp.full_like(need, -1)
    hi0 = jnp.full_like(need, idx_hi)
    _, _, hi = lax.while_loop(cond, body, (jnp.int32(0), lo0, hi0))
    return hi


def _attn_prompt_kernel(qt_ref, qit_ref, wit_ref, k_ref, vt_ref, ki_ref, bias_ref, bfar_ref, yb_o,
                        s_scr, s16_scr, o_scr, qm_scr, m_scr, l_scr, a_scr, off_scr, x_scr, *, k_sel):
    i = pl.program_id(1)
    n_tiles = i + 1
    tq = Q_TILE
    tk = KV_TILE
    q_pos = i * tq + lax.broadcasted_iota(I32, (1, tq), 1)
    kv_iota = lax.broadcasted_iota(I32, (tk, tq), 0)

    n_pairs = (n_tiles + 1) // 2

    def score_pair(jj, top):
        for half in range(2):
            r0 = pl.multiple_of((2 * jj + half) * tk, tk)
            ki_t = ki_ref[0, pl.ds(r0, tk), :]
            acc = jnp.zeros((tk, tq), F32)
            for h in range(IDX_HEADS):
                s = _dot(ki_t, qit_ref[0, h * IDX_DIM:(h + 1) * IDX_DIM, :])
                acc = acc + wit_ref[0, h:h + 1, :] * jnp.maximum(s, 0.0)
            acc = jnp.where(kv_iota + r0 <= q_pos, acc, -jnp.inf)
            key = _ordered_key(acc)
            s_scr[pl.ds(r0, tk), :] = key
            s16_scr[pl.ds(r0, tk), :] = (key >> 16).astype(jnp.int16)
            top = jnp.maximum(top, jnp.max(acc, axis=0, keepdims=True))
        return top

    top = lax.fori_loop(0, n_pairs, score_pair, jnp.full((1, tq), -jnp.inf, F32))

    def count_tiles(pred):
        def tile(jj, acc):
            for half in range(2):
                r0 = pl.multiple_of((2 * jj + half) * tk, tk)
                ind = pred(s_scr[pl.ds(r0, tk), :], r0).astype(I32)
                for r in range(tk // 8):
                    acc = acc + ind[r * 8:(r + 1) * 8]
            return acc
        acc = lax.fori_loop(0, n_pairs, tile, jnp.zeros((8, tq), I32))
        return jnp.sum(acc, axis=0, keepdims=True)

    def count_ge_high(mid):
        mid16 = mid.astype(jnp.int16)

        def tile(jj, acc):
            r0 = pl.multiple_of(jj * 2 * tk, 2 * tk)
            ind = jnp.where(s16_scr[pl.ds(r0, 2 * tk), :] >= mid16, jnp.int16(1), jnp.int16(0))
            for r in range(2 * tk // 16):
                acc = acc + ind[r * 16:(r + 1) * 16]
            return acc
        acc = lax.fori_loop(0, n_pairs, tile, jnp.zeros((16, tq), jnp.int16))
        return jnp.sum(acc.astype(I32), axis=0, keepdims=True)

    count_ge = lambda mid: count_tiles(lambda t, r0: t >= mid)
    n_valid = q_pos + 1
    searching = n_valid > k_sel
    zeros = jnp.zeros((1, tq), I32)
    floor_high = jnp.full((1, tq), (KEY_NEG_INF + 1) >> 16, I32)
    top_high = _ordered_key(top) >> 16
    guess = jnp.maximum(top_high - (4 << 7), 0)
    c_guess = count_ge_high(guess)
    use_guess = jnp.logical_and(jnp.logical_and(searching, top_high > 0), c_guess >= k_sel)
    lo_high = jnp.where(use_guess, guess, floor_high)
    c_lo_high = jnp.where(use_guess, c_guess, n_valid)
    hi_high = jnp.where(searching, top_high + 1, floor_high + 1)
    bucket, c_bucket, c_next = _bisect_kth_fixed(count_ge_high, lo_high, hi_high, c_lo_high, zeros, k_sel)
    settled = jnp.logical_or(jnp.logical_not(searching), c_bucket == k_sel)
    lo0 = bucket << 16
    bucket_end = jnp.where(bucket >= 32767, KEY_MAX, (bucket + 1) << 16)
    hi0 = jnp.where(settled, lo0 + 1, bucket_end)
    thr, c_thr, c_above = _bisect_kth(count_ge, lo0, hi0, c_bucket, c_next, k_sel, steps_per_check=2)
    has_tie = jnp.logical_and(searching, c_thr > k_sel)
    need = k_sel - c_above
    n_tie_rows = jnp.max(has_tie.astype(I32))

    @pl.when(n_tie_rows > 0)
    def _():
        def mark(jj, _):
            r0 = pl.multiple_of(jj * 2 * tk, 2 * tk)
            eq = s_scr[pl.ds(r0, 2 * tk), :] == thr
            s16_scr[pl.ds(r0, 2 * tk), :] = jnp.where(eq, 1, 0).astype(jnp.int16)
            return 0
        lax.fori_loop(0, n_pairs, mark, 0)

    iota16 = lax.broadcasted_iota(I32, (2 * tk, tq), 0).astype(jnp.int16)

    def count_tie_le(mid):
        def tile(jj, acc):
            r0 = pl.multiple_of(jj * 2 * tk, 2 * tk)
            local = jnp.clip(mid - r0, -1, 2 * tk).astype(jnp.int16)
            ind = jnp.where(iota16 <= local, s16_scr[pl.ds(r0, 2 * tk), :], jnp.int16(0))
            for r in range(2 * tk // 16):
                acc = acc + ind[r * 16:(r + 1) * 16]
            return acc
        acc = lax.fori_loop(0, n_pairs, tile, jnp.zeros((16, tq), jnp.int16))
        return jnp.sum(acc.astype(I32), axis=0, keepdims=True)

    tie_j = _bisect_tie_index(count_tie_le, need, n_tie_rows, n_tiles * tk - 1, 14)
    tie_j = jnp.where(has_tie, tie_j, KEY_MAX)

    def mask_pair(jj, _):
        for half in range(2):
            r0 = pl.multiple_of((2 * jj + half) * tk, tk)
            t = s_scr[pl.ds(r0, tk), :]
            tie_ok = jnp.where(kv_iota + r0 <= tie_j, 0.0, NEG_BIG)
            madd = jnp.where(t > thr, 0.0, jnp.where(t == thr, tie_ok, NEG_BIG))
            s_scr[pl.ds(r0, tk), :] = pltpu.bitcast(madd.astype(F32), I32)
        return 0

    lax.fori_loop(0, n_pairs, mask_pair, 0)

    row = lax.broadcasted_iota(I32, (2 * HEAD_DIM, tq), 0)
    for pair in range(N_HEADS // 2):
        q_pair = qt_ref[0, pair * 128:(pair + 1) * 128, :]
        zero = jnp.zeros_like(q_pair)
        qm_scr[2 * pair] = jnp.where(row < HEAD_DIM, q_pair, zero)
        qm_scr[2 * pair + 1] = jnp.where(row >= HEAD_DIM, q_pair, zero)
    m_scr[...] = jnp.full(m_scr.shape, NEG_BIG, F32)
    l_scr[...] = jnp.zeros(l_scr.shape, F32)
    o_scr[...] = jnp.zeros(o_scr.shape, F32)

    def attn_tiles(tiles):
        for slot, (j, near_kind) in enumerate(tiles):
            r0 = pl.multiple_of(j * tk, tk)
            for pair in range(N_HEADS // 2):
                k_t = k_ref[0, pl.ds(r0, tk), pair * 128:(pair + 1) * 128]
                for h in (2 * pair, 2 * pair + 1):
                    x = _dot(k_t, qm_scr[h]) + pltpu.bitcast(s_scr[pl.ds(r0, tk), :], F32)
                    if near_kind is None:
                        c = bfar_ref[h]
                    else:
                        x = x + bias_ref[h, near_kind]
                        c = 0.0
                    x_scr[slot, h] = x
                    m = m_scr[h]
                    m_new = jnp.maximum(m, jnp.max(x, axis=0, keepdims=True) + c)
                    a_scr[slot, h] = jnp.exp2(m - m_new)
                    off_scr[slot, h] = m_new - c
                    m_scr[h] = m_new
        for slot, (j, _) in enumerate(tiles):
            r0 = pl.multiple_of(j * tk, tk)
            for h in range(N_HEADS):
                alpha = a_scr[slot, h]
                p = jnp.exp2(x_scr[slot, h] - off_scr[slot, h])
                l_scr[h] = alpha * l_scr[h] + jnp.sum(p, axis=0, keepdims=True)
                v_t = vt_ref[0, h * HEAD_DIM:(h + 1) * HEAD_DIM, pl.ds(r0, tk)]
                hs = slice(h * HEAD_DIM, (h + 1) * HEAD_DIM)
                o_scr[hs, :] = alpha * o_scr[hs, :] + _dot(v_t, p.astype(BF16))

    n_far = jnp.maximum(i - 1, 0)

    def far_pair(jj, _):
        attn_tiles([(2 * jj, None), (2 * jj + 1, None)])
        return 0

    lax.fori_loop(0, n_far // 2, far_pair, 0)

    @pl.when(n_far % 2 == 1)
    def _():
        attn_tiles([(n_far - 1, None)])

    @pl.when(i >= 1)
    def _():
        attn_tiles([(i - 1, 1), (i, 0)])

    @pl.when(i == 0)
    def _():
        attn_tiles([(i, 0)])

    for h in range(N_HEADS):
        hs = slice(h * HEAD_DIM, (h + 1) * HEAD_DIM)
        o_scr[hs, :] = o_scr[hs, :] / l_scr[h]
    yb_o[0] = o_scr[...].T.astype(BF16)


def _attn_prompt(qt, qit, wit, k_bf, vt, ki_bf, bias_t, bias_far):
    bsz, _, s_len = qt.shape
    assert s_len % (2 * KV_TILE) == 0 and Q_TILE == KV_TILE
    qblk = lambda c: pl.BlockSpec((1, c, Q_TILE), lambda b, i: (b, 0, i))
    return pl.pallas_call(
        functools.partial(_attn_prompt_kernel, k_sel=min(TOPK_MAX, s_len // 4)),
        grid=(bsz, s_len // Q_TILE),
        in_specs=[qblk(ATT_WIDTH), qblk(IDX_HEADS * IDX_DIM), qblk(IDX_HEADS),
                  _resident_spec((1, s_len, ATT_WIDTH), lambda b, i: (b, 0, 0)),
                  _resident_spec((1, ATT_WIDTH, s_len), lambda b, i: (b, 0, 0)),
                  _resident_spec((1, s_len, IDX_DIM), lambda b, i: (b, 0, 0)),
                  _resident_spec(bias_t.shape, lambda b, i: (0, 0, 0, 0)),
                  pl.BlockSpec(memory_space=pltpu.SMEM)],
        out_specs=pl.BlockSpec((1, Q_TILE, ATT_WIDTH), lambda b, i: (b, i, 0)),
        out_shape=jax.ShapeDtypeStruct((bsz, s_len, ATT_WIDTH), BF16),
        scratch_shapes=[pltpu.VMEM((s_len, Q_TILE), I32), pltpu.VMEM((s_len, Q_TILE), jnp.int16),
                        pltpu.VMEM((ATT_WIDTH, Q_TILE), F32),
                        pltpu.VMEM((N_HEADS, 2 * HEAD_DIM, Q_TILE), BF16),
                        pltpu.VMEM((N_HEADS, 1, Q_TILE), F32), pltpu.VMEM((N_HEADS, 1, Q_TILE), F32),
                        pltpu.VMEM((2, N_HEADS, 1, Q_TILE), F32), pltpu.VMEM((2, N_HEADS, 1, Q_TILE), F32),
                        pltpu.VMEM((2, N_HEADS, KV_TILE, Q_TILE), F32)],
        compiler_params=_params(("parallel", "arbitrary")),
        name="attn_prompt",
    )(qt, qit, wit, k_bf, vt, ki_bf, bias_t, bias_far)


def _sample_scores_kernel(pt_ref, qi_ref, wi_ref, kin_ref, *rest, gp):
    pages = rest[:gp]
    keys_o, newkey_o = rest[gp:]
    qi = qi_ref[0]
    wi = wi_ref[0]
    rows = []
    for g in range(gp):
        s = _dot(qi, pages[g][0].astype(BF16))
        rows.append(jnp.sum(wi * jnp.maximum(s, 0.0), axis=0, keepdims=True))
    keys_o[0] = _ordered_key(jnp.concatenate(rows, axis=0))

    @pl.when(pl.program_id(1) == 0)
    def _():
        s_new = jnp.sum(qi.astype(F32) * kin_ref[0].astype(F32), axis=1, keepdims=True)
        key_new = _ordered_key(jnp.sum(wi * jnp.maximum(s_new, 0.0), axis=0, keepdims=True))
        newkey_o[0] = jnp.broadcast_to(key_new, (1, PAGE_SIZE))


def _sample_scores(page_table, qi, wi, ki_new, cache_idx_t):
    n, n_pages = page_table.shape
    gp = min(SAMPLE_IDX_PAGES, n_pages)
    assert n_pages % gp == 0
    page_spec = lambda g: pl.BlockSpec((1, IDX_DIM, PAGE_SIZE), lambda b, c, pt: (pt[b, c * gp + g], 0, 0))
    per_seq = lambda shape: pl.BlockSpec((1,) + shape, lambda b, c, pt: (b, 0, 0))
    grid_spec = pltpu.PrefetchScalarGridSpec(
        num_scalar_prefetch=1,
        grid=(n, n_pages // gp),
        in_specs=[per_seq((IDX_HEADS, IDX_DIM)), per_seq((IDX_HEADS, 1)), per_seq((1, IDX_DIM))]
        + [page_spec(g) for g in range(gp)],
        out_specs=[pl.BlockSpec((1, gp, PAGE_SIZE), lambda b, c, pt: (b, c, 0)), per_seq((1, PAGE_SIZE))],
    )
    return pl.pallas_call(
        functools.partial(_sample_scores_kernel, gp=gp),
        grid_spec=grid_spec,
        out_shape=[jax.ShapeDtypeStruct((n, n_pages, PAGE_SIZE), I32), jax.ShapeDtypeStruct((n, 1, PAGE_SIZE), I32)],
        compiler_params=_params(("parallel", "arbitrary")),
        name="sample_scores",
    )(page_table, qi, wi, ki_new, *([cache_idx_t] * gp))


def _sample_select_kernel(keys_ref, newkey_ref, madd_o, newm_o, *, k_sel):
    n, past = keys_ref.shape
    key_new = newkey_ref[:, :1]
    idx = lax.broadcasted_iota(I32, (n, past), 1)

    def total(ind_past, ind_new):
        return jnp.sum(ind_past.astype(I32), axis=1, keepdims=True) + ind_new.astype(I32)

    lo0 = jnp.full((n, 1), KEY_NEG_INF + 1, I32)
    hi0 = jnp.full((n, 1), KEY_MAX, I32)
    thr, c_thr, c_above = _bisect_kth(lambda mid: total(keys_ref[...] >= mid, key_new >= mid),
                                      lo0, hi0, jnp.full((n, 1), past + 1, I32), jnp.zeros((n, 1), I32), k_sel)
    has_tie = c_thr > k_sel
    need = k_sel - c_above
    tie_j = _bisect_tie_index(
        lambda mid: total(jnp.logical_and(keys_ref[...] == thr, idx <= mid),
                          jnp.logical_and(key_new == thr, past <= mid)),
        need, jnp.max(has_tie.astype(I32)), past, 14)
    tie_j = jnp.where(has_tie, tie_j, KEY_MAX)
    t = keys_ref[...]
    tie_ok = jnp.where(idx <= tie_j, 0.0, NEG_BIG)
    madd_o[...] = jnp.where(t > thr, 0.0, jnp.where(t == thr, tie_ok, NEG_BIG)).astype(F32)
    new_ok = jnp.where(past <= tie_j, 0.0, NEG_BIG)
    newm = jnp.where(key_new > thr, 0.0, jnp.where(key_new == thr, new_ok, NEG_BIG)).astype(F32)
    newm_o[...] = jnp.broadcast_to(newm, (n, PAGE_SIZE))


def _sample_select(keys, newkey):
    n, past = keys.shape
    k_sel = min(TOPK_MAX, (past + 1) // 4)
    return pl.pallas_call(
        functools.partial(_sample_select_kernel, k_sel=k_sel),
        grid=(1,),
        in_specs=[_const_spec(keys.shape), _const_spec(newkey.shape)],
        out_specs=[_const_spec(keys.shape), _const_spec(newkey.shape)],
        out_shape=[jax.ShapeDtypeStruct(keys.shape, F32), jax.ShapeDtypeStruct(newkey.shape, F32)],
        compiler_params=_params(("arbitrary",)),
        name="sample_select",
    )(keys, newkey)


def _attn_sample_kernel(pt_ref, q_ref, kn_ref, vn_ref, madd_ref, newm_ref, bias_ref, biasn_ref, *rest, n_pages, gp):
    k_pages = rest[:gp]
    v_pages = rest[gp:2 * gp]
    yb_o, m_scr, l_scr, acc_scr = rest[2 * gp:]
    c = pl.program_id(1)
    n_chunks = n_pages // gp
    head_of_lane = lax.broadcasted_iota(I32, (N_HEADS, ATT_WIDTH), 1) // HEAD_DIM
    head_of_row = lax.broadcasted_iota(I32, (N_HEADS, ATT_WIDTH), 0)
    own = head_of_lane == head_of_row
    q_blk = jnp.where(own, jnp.broadcast_to(q_ref[0], (N_HEADS, ATT_WIDTH)), 0.0)

    @pl.when(c == 0)
    def _():
        kn = kn_ref[0].astype(BF16).astype(F32)
        x_new = (jnp.sum(q_blk.astype(BF16).astype(F32) * kn, axis=1, keepdims=True)
                 + biasn_ref[...] + newm_ref[0][:, :1])
        m_scr[...] = x_new
        l_scr[...] = jnp.ones_like(l_scr)
        acc_scr[...] = jnp.broadcast_to(vn_ref[0].astype(BF16).astype(F32), (N_HEADS, ATT_WIDTH))

    q_bf = q_blk.astype(BF16)
    xs = []
    for g in range(gp):
        pg = c * gp + g
        kt = k_pages[g][0].astype(BF16)
        xs.append(_dot(q_bf, kt) + madd_ref[0, pg] + bias_ref[pg])
    x = jnp.concatenate(xs, axis=1)
    m = m_scr[...]
    m_new = jnp.maximum(m, jnp.max(x, axis=1, keepdims=True))
    alpha = jnp.exp2(m - m_new)
    p = jnp.exp2(x - m_new)
    l_scr[...] = alpha * l_scr[...] + jnp.sum(p, axis=1, keepdims=True)
    p_bf = p.astype(BF16)
    pv = jnp.zeros((N_HEADS, ATT_WIDTH), F32)
    for g in range(gp):
        vt = v_pages[g][0].astype(BF16)
        pv = pv + _dot_nt(p_bf[:, g * PAGE_SIZE:(g + 1) * PAGE_SIZE], vt)
    acc_scr[...] = alpha * acc_scr[...] + pv
    m_scr[...] = m_new

    @pl.when(c == n_chunks - 1)
    def _():
        o = acc_scr[...] / l_scr[...]
        yb_o[0] = jnp.sum(jnp.where(own, o, 0.0), axis=0, keepdims=True).astype(BF16)


def _attn_sample(page_table, q, k_new, v_new, madd, newm, bias_pages, bias_new, cache_kt, cache_vt):
    n, n_pages = page_table.shape
    gp = min(SAMPLE_KV_PAGES, n_pages)
    assert n_pages % gp == 0
    page_spec = lambda g: pl.BlockSpec((1, ATT_WIDTH, PAGE_SIZE), lambda b, c, pt: (pt[b, c * gp + g], 0, 0))
    per_seq = lambda shape: pl.BlockSpec((1,) + shape, lambda b, c, pt: (b, 0, 0))
    grid_spec = pltpu.PrefetchScalarGridSpec(
        num_scalar_prefetch=1,
        grid=(n, n_pages // gp),
        in_specs=[per_seq((1, ATT_WIDTH)), per_seq((1, ATT_WIDTH)), per_seq((1, ATT_WIDTH)),
                  pl.BlockSpec((1, n_pages, 1, PAGE_SIZE), lambda b, c, pt: (b, 0, 0, 0)), per_seq((1, PAGE_SIZE)),
                  pl.BlockSpec(bias_pages.shape, lambda b, c, pt: (0, 0, 0)),
                  pl.BlockSpec(bias_new.shape, lambda b, c, pt: (0, 0))]
        + [page_spec(g) for g in range(gp)] * 2,
        out_specs=per_seq((1, ATT_WIDTH)),
        scratch_shapes=[pltpu.VMEM((N_HEADS, 1), F32), pltpu.VMEM((N_HEADS, 1), F32),
                        pltpu.VMEM((N_HEADS, ATT_WIDTH), F32)],
    )
    return pl.pallas_call(
        functools.partial(_attn_sample_kernel, n_pages=n_pages, gp=gp),
        grid_spec=grid_spec,
        out_shape=jax.ShapeDtypeStruct((n, 1, ATT_WIDTH), BF16),
        compiler_params=_params(("parallel", "arbitrary")),
        name="attn_sample",
    )(page_table, q, k_new, v_new, madd, newm, bias_pages, bias_new, *([cache_kt] * gp), *([cache_vt] * gp))


def _merge_kernel(x_ref, ya_ref, yb_ref, sga_ref, sgb_ref, wa, wb, wo, g_ref, b_ref, x1_o, *, alpha):
    merged = (sga_ref[...].astype(F32) * _dot(ya_ref[...], wa[...])
              + sgb_ref[...].astype(F32) * _dot(yb_ref[...], wb[...]))
    z = alpha * x_ref[...] + _dot(merged.astype(BF16), wo[...])
    x1_o[...] = _layer_norm(z, g_ref[...], b_ref[...])


def _merge(x, ya, yb, sga, sgb, wa, wb, wo, g, b, alpha):
    n = x.shape[0]
    tm = min(TOKEN_TILE, n)
    rows = lambda c: pl.BlockSpec((tm, c), lambda i: (i, 0))
    return pl.pallas_call(
        functools.partial(_merge_kernel, alpha=alpha),
        grid=(n // tm,),
        in_specs=[rows(D_MODEL), rows(SSM_WIDTH), rows(ATT_WIDTH), rows(D_MODEL), rows(D_MODEL),
                  _const_spec(wa.shape), _const_spec(wb.shape), _const_spec(wo.shape),
                  _const_spec(g.shape), _const_spec(b.shape)],
        out_specs=rows(D_MODEL),
        out_shape=jax.ShapeDtypeStruct((n, D_MODEL), F32),
        compiler_params=_params(("parallel",)),
        name="merge_out_proj",
    )(x, ya, yb, sga, sgb, wa, wb, wo, g, b)


def _ffn_kernel(x1_ref, wg, wu, wd, g_ref, b_ref, y_o, acc_scr, *, alpha):
    x1 = x1_ref[...]
    xb = x1.astype(BF16)
    acc_scr[...] = alpha * x1
    for c in range(D_FF // FF_CHUNK):
        sl = slice(c * FF_CHUNK, (c + 1) * FF_CHUNK)
        gate = _dot(xb, wg[:, sl])
        up = _dot(xb, wu[:, sl])
        hid = (jax.nn.silu(gate) * up).astype(BF16)
        acc_scr[...] += _dot(hid, wd[sl, :])
    y_o[...] = _layer_norm(acc_scr[...], g_ref[...], b_ref[...])


def _ffn(x1, wg, wu, wd, g, b, alpha):
    n = x1.shape[0]
    tm = min(TOKEN_TILE, n)
    rows = pl.BlockSpec((tm, D_MODEL), lambda i: (i, 0))
    return pl.pallas_call(
        functools.partial(_ffn_kernel, alpha=alpha),
        grid=(n // tm,),
        in_specs=[rows, _const_spec(wg.shape), _const_spec(wu.shape), _const_spec(wd.shape),
                  _const_spec(g.shape), _const_spec(b.shape)],
        out_specs=rows,
        out_shape=jax.ShapeDtypeStruct((n, D_MODEL), F32),
        scratch_shapes=[pltpu.VMEM((tm, D_MODEL), F32)],
        compiler_params=_params(("parallel",)),
        name="ffn",
    )(x1, wg, wu, wd, g, b)


def _rel_bucket(rel):
    n = jnp.maximum(rel, 0)
    max_exact = REL_BUCKETS // 2
    nf = jnp.maximum(n, max_exact).astype(F32)
    large = max_exact + (jnp.log(nf / max_exact) / math.log(REL_MAX_DIST / max_exact)
                         * (REL_BUCKETS - max_exact)).astype(I32)
    large = jnp.minimum(large, REL_BUCKETS - 1)
    return jnp.where(n < max_exact, n, large)


def _bias_lookup(rel_bias, rel):
    onehot = (_rel_bucket(rel)[..., None] == jnp.arange(REL_BUCKETS, dtype=I32)).astype(F32)
    table = rel_bias.astype(F32) * LOG2E
    return jnp.einsum('...k,kh->...h', onehot, table, precision=lax.Precision.HIGHEST)


def _prompt_bias_tiles(rel_bias):
    kl = jnp.arange(KV_TILE, dtype=I32)[:, None]
    ql = jnp.arange(Q_TILE, dtype=I32)[None, :]
    rel = jnp.stack([ql - kl, ql - kl + KV_TILE])
    assert KV_TILE + 1 >= REL_MAX_DIST
    near = jnp.moveaxis(_bias_lookup(rel_bias, rel), -1, 0)
    far = _bias_lookup(rel_bias, jnp.full((), 2 * KV_TILE, I32))
    return near, far


def _ssm_operators(a_re, a_im, log_dt, b_re, b_im, c_re, c_im, d):
    g, p, gi = SSM_GROUPS, SSM_STATE, SSM_GROUP
    a_re, a_im = a_re.astype(F32), a_im.astype(F32)
    dt = jnp.exp(log_dt.astype(F32))[:, None]
    mag = jnp.exp(dt * a_re)
    lre = mag * jnp.cos(dt * a_im)
    lim = mag * jnp.sin(dt * a_im)
    den = a_re * a_re + a_im * a_im
    fr = ((lre - 1.0) * a_re + lim * a_im) / den
    fi = (lim * a_re - (lre - 1.0) * a_im) / den
    bbr = fr[:, :, None] * b_re - fi[:, :, None] * b_im
    bbi = fr[:, :, None] * b_im + fi[:, :, None] * b_re
    eye = jnp.eye(g, dtype=F32)
    blockdiag_in = lambda m: jnp.einsum('gpi,gh->gihp', m, eye).reshape(g * gi, g * p)
    blockdiag_out = lambda m: jnp.einsum('gip,gh->gphi', m, eye).reshape(g * p, g * gi)
    bfull = jnp.concatenate([blockdiag_in(bbr), blockdiag_in(bbi)], axis=1)
    cfull = jnp.concatenate([blockdiag_out(c_re.astype(F32)), blockdiag_out(-c_im.astype(F32))], axis=0)
    sc = SLAB_COLS
    bslab = jnp.stack([jnp.concatenate([bfull[s * 128:(s + 1) * 128, s * sc:(s + 1) * sc],
                                        bfull[s * 128:(s + 1) * 128, g * p + s * sc:g * p + (s + 1) * sc]], axis=1)
                       for s in range(SSM_SLABS)])
    cslab = jnp.stack([jnp.concatenate([cfull[s * sc:(s + 1) * sc, s * 128:(s + 1) * 128],
                                        cfull[g * p + s * sc:g * p + (s + 1) * sc, s * 128:(s + 1) * 128]], axis=0)
                       for s in range(SSM_SLABS)])
    return {'lre': lre.reshape(1, g * p), 'lim': lim.reshape(1, g * p),
            'bfull': bfull.astype(BF16), 'cfull': cfull.astype(BF16),
            'bslab': bslab.astype(BF16), 'cslab': cslab.astype(BF16), 'd': d.astype(F32).reshape(1, g * gi)}


def _split_w_in(w_in):
    offs = [0]
    for n in IN_SPLITS:
        offs.append(offs[-1] + n)
    names = ['u', 'q', 'k', 'v', 'qi', 'ki', 'wi', 'ga', 'gb']
    return {nm: w_in[:, offs[j]:offs[j + 1]] for j, nm in enumerate(names)}


def kernel(x_prompt, x_sample, cache_k, cache_v, cache_idx_k, state_ssm_re, state_ssm_im, page_table, w_in, ssm_a_re, ssm_a_im, ssm_log_dt, ssm_b_re, ssm_b_im, ssm_c_re, ssm_c_im, ssm_d, w_glu, w_branch_a, w_branch_b, rel_bias, w_out, ln1_g, ln1_b, w_ffn_in, w_ffn_out, ln2_g, ln2_b):
    depth = w_in.shape[0]
    bsz, s_len, _ = x_prompt.shape
    n_dec, dec_seq, _ = x_sample.shape
    assert dec_seq == 1
    n_pages = page_table.shape[1]
    n_pool = cache_k.shape[1]
    past = n_pages * PAGE_SIZE
    alpha = (2 * depth) ** 0.25
    idx_scale = IDX_DIM ** -0.5 * IDX_HEADS ** -0.5
    att_scale = HEAD_DIM ** -0.5 * LOG2E

    bias_t, bias_far = _prompt_bias_tiles(rel_bias)
    key_pos = jnp.arange(past, dtype=I32).reshape(n_pages, PAGE_SIZE)
    bias_pages = jnp.swapaxes(_bias_lookup(rel_bias, past - key_pos), 1, 2)
    bias_new = _bias_lookup(rel_bias, jnp.zeros((), I32)).reshape(N_HEADS, 1)

    x_p = x_prompt
    x_s = x_sample.reshape(n_dec, D_MODEL)
    outs = {k: [] for k in ('kp', 'vp', 'kip', 'hpr', 'hpi', 'ks', 'vs', 'kis', 'hsr', 'hsi')}
    for l in range(depth):
        w = _split_w_in(w_in[l])
        wn = {nm: w[nm].astype(BF16) for nm in ('u', 'k', 'v', 'ki', 'ga', 'gb')}
        wn['tq'] = (w['q'] * att_scale).T.astype(BF16)
        wn['tv'] = w['v'].T.astype(BF16)
        wn['tqi'] = w['qi'].T.astype(BF16)
        wn['twi'] = jnp.pad((w['wi'] * idx_scale).T, ((0, 16 - IDX_HEADS), (0, 0))).astype(BF16)
        ws = {nm: w[nm].astype(BF16) for nm in ('u', 'k', 'v', 'qi', 'ki', 'ga', 'gb')}
        ws['q'] = (w['q'] * att_scale).astype(BF16)
        ws['wi'] = (w['wi'] * idx_scale).astype(BF16)
        ops = _ssm_operators(ssm_a_re[l], ssm_a_im[l], ssm_log_dt[l], ssm_b_re[l], ssm_b_im[l],
                             ssm_c_re[l], ssm_c_im[l], ssm_d[l])
        wglu = w_glu[l].astype(BF16)
        wa, wb, wo = w_branch_a[l].astype(BF16), w_branch_b[l].astype(BF16), w_out[l].astype(BF16)
        wg, wu = w_ffn_in[l][:, :D_FF].astype(BF16), w_ffn_in[l][:, D_FF:].astype(BF16)
        wd = w_ffn_out[l].astype(BF16)
        g1, b1 = ln1_g[l].astype(F32).reshape(1, D_MODEL), ln1_b[l].astype(F32).reshape(1, D_MODEL)
        g2, b2 = ln2_g[l].astype(F32).reshape(1, D_MODEL), ln2_b[l].astype(F32).reshape(1, D_MODEL)

        u, k, v, ki, k_bf, ki_bf, sga, sgb, qt, vt, qit, wit = _in_proj_prompt(x_p, wn)
        ya, hpr, hpi = _ssm_prompt(u, ops, wglu)
        yb = _attn_prompt(qt, qit, wit, k_bf, vt, ki_bf, bias_t, bias_far)
        n_tok = bsz * s_len
        flat = lambda a: a.reshape(n_tok, a.shape[-1])
        x1 = _merge(flat(x_p), flat(ya), flat(yb), flat(sga), flat(sgb), wa, wb, wo, g1, b1, alpha)
        x_p = _ffn(x1, wg, wu, wd, g2, b2, alpha).reshape(bsz, s_len, D_MODEL)
        outs['kp'].append(k.reshape(bsz, s_len, N_HEADS, HEAD_DIM))
        outs['vp'].append(v.reshape(bsz, s_len, N_HEADS, HEAD_DIM))
        outs['kip'].append(ki)
        outs['hpr'].append(hpr.reshape(bsz, SSM_GROUPS, SSM_STATE))
        outs['hpi'].append(hpi.reshape(bsz, SSM_GROUPS, SSM_STATE))

        us, qs, k_s, v_s, qis, kis, wis, sga_s, sgb_s = _in_proj_sample(x_s, ws)
        ya_s, hsr, hsi = _ssm_sample(us, state_ssm_re[l].reshape(n_dec, SSM_COLS).astype(F32),
                                     state_ssm_im[l].reshape(n_dec, SSM_COLS).astype(F32), ops, wglu)
        cache_kt = jnp.transpose(cache_k[l], (0, 2, 3, 1)).reshape(n_pool, ATT_WIDTH, PAGE_SIZE)
        cache_vt = jnp.transpose(cache_v[l], (0, 2, 3, 1)).reshape(n_pool, ATT_WIDTH, PAGE_SIZE)
        cache_it = jnp.transpose(cache_idx_k[l], (0, 2, 1))
        keys, newkey = _sample_scores(
            page_table, qis.astype(BF16).reshape(n_dec, IDX_HEADS, IDX_DIM), wis.reshape(n_dec, IDX_HEADS, 1),
            kis.astype(BF16).reshape(n_dec, 1, IDX_DIM), cache_it)
        madd, newm = _sample_select(keys.reshape(n_dec, past), newkey.reshape(n_dec, PAGE_SIZE))
        yb_s = _attn_sample(page_table, qs.reshape(n_dec, 1, ATT_WIDTH), k_s.reshape(n_dec, 1, ATT_WIDTH),
                            v_s.reshape(n_dec, 1, ATT_WIDTH), madd.reshape(n_dec, n_pages, 1, PAGE_SIZE),
                            newm.reshape(n_dec, 1, PAGE_SIZE), bias_pages, bias_new, cache_kt, cache_vt)
        x1_s = _merge(x_s, ya_s, yb_s.reshape(n_dec, ATT_WIDTH), sga_s, sgb_s, wa, wb, wo, g1, b1, alpha)
        x_s = _ffn(x1_s, wg, wu, wd, g2, b2, alpha)
        outs['ks'].append(k_s.reshape(n_dec, 1, N_HEADS, HEAD_DIM))
        outs['vs'].append(v_s.reshape(n_dec, 1, N_HEADS, HEAD_DIM))
        outs['kis'].append(kis.reshape(n_dec, 1, IDX_DIM))
        outs['hsr'].append(hsr.reshape(n_dec, SSM_GROUPS, SSM_STATE))
        outs['hsi'].append(hsi.reshape(n_dec, SSM_GROUPS, SSM_STATE))

    st = lambda name: jnp.stack(outs[name])
    return (x_p, x_s.reshape(n_dec, 1, D_MODEL),
            st('kp'), st('vp'), st('kip'), st('hpr'), st('hpi'),
            st('ks'), st('vs'), st('kis'), st('hsr'), st('hsi'))
```
